```python
import jax, jax.numpy as jnp
from jax import lax
import numpy as np

D_MODEL = 2048
BATCH = 2
SEQ = 4096
DEPTH = 2
DEC_BATCH = 128
DEC_SEQ = 4
PAST_LEN = 2048
PAGE_SIZE = 128

CONV_W = D_MODEL // 2
CONV_K = 31
CONV_GROUPS = 8
MLSTM_HEADS = 4
MLSTM_W = D_MODEL // 2
MLSTM_DH = MLSTM_W // MLSTM_HEADS
MLSTM_CHUNK = 128
MLSTM_FORGET_BIAS = 3.0
IN0_COLS = 2 * CONV_W + 4 * MLSTM_W + 2 * MLSTM_HEADS
FOX_HEADS = 16
FOX_DH = D_MODEL // FOX_HEADS
FOX_W = FOX_HEADS * FOX_DH
FOX_FORGET_BIAS = 3.0
Q_BLOCK = 128
IN1_COLS = 3 * FOX_W + FOX_HEADS
PEER_HEADS = 8
PEER_NKEYS = 128
PEER_EXPERTS = PEER_NKEYS * PEER_NKEYS
PEER_DK = 128
PEER_TOPK = 16
PEER_TOKEN_BLOCK = 256
RMS_EPS = 1e-6
LN_EPS = 1e-5

kernel_name = 'hybrid_conv_mlstm_fox_peer_step'

F32 = jnp.float32


def rms_norm(x, g):
    xf = x.astype(F32)
    y = xf * lax.rsqrt(jnp.mean(jnp.square(xf), axis=-1, keepdims=True) + RMS_EPS)
    return (y * g.astype(F32)).astype(x.dtype)


def group_layer_norm(x, g, b):
    xf = x.astype(F32).reshape(x.shape[:-1] + (CONV_GROUPS, -1))
    mu = jnp.mean(xf, axis=-1, keepdims=True)
    var = jnp.mean(jnp.square(xf - mu), axis=-1, keepdims=True)
    y = ((xf - mu) * lax.rsqrt(var + LN_EPS)).reshape(x.shape)
    return (y * g.astype(F32) + b.astype(F32)).astype(x.dtype)


def mlstm_chunk(carry, inp):
    c0, n0, m0 = carry
    q, k, v, ig, lf = inp
    L = q.shape[2]
    f_cum = jnp.cumsum(lf, axis=-1)
    g = ig - f_cum
    m = f_cum + jnp.maximum(m0[..., None], lax.cummax(g, axis=2))
    a = jnp.exp(f_cum + m0[..., None] - m)
    causal = jnp.tril(jnp.ones((L, L), dtype=bool))
    log_d = (f_cum - m)[..., :, None] + g[..., None, :]
    d = jnp.exp(jnp.where(causal, log_d, -jnp.inf))
    qk = jnp.einsum('bhtd,bhsd->bhts', q, k) * d
    num = a[..., None] * jnp.einsum('bhtd,bhde->bhte', q, c0) + jnp.einsum('bhts,bhse->bhte', qk, v)
    den = a * jnp.einsum('bhtd,bhd->bht', q, n0) + jnp.sum(qk, axis=-1)
    h = num / jnp.maximum(jnp.abs(den), jnp.exp(-m))[..., None]
    d_last = d[..., -1, :]
    c1 = a[..., -1, None, None] * c0 + jnp.einsum('bhs,bhsd,bhse->bhde', d_last, k, v)
    n1 = a[..., -1, None] * n0 + jnp.einsum('bhs,bhsd->bhd', d_last, k)
    return (c1, n1, m[..., -1]), h


def conv_mlstm_mixer(h, conv_buf, c0, n0, m0, w_in, b_ig, b_fg, conv_w, conv_b, cn_g, cn_b, mh_g, w_out):
    bsz, s_len, _ = h.shape
    cw, mw, nh = CONV_W, MLSTM_W, MLSTM_HEADS
    p = h @ w_in
    c_val = p[..., :cw]
    c_gate = p[..., cw:2 * cw]
    q = p[..., 2 * cw:2 * cw + mw]
    k = p[..., 2 * cw + mw:2 * cw + 2 * mw]
    v = p[..., 2 * cw + 2 * mw:2 * cw + 3 * mw]
    o = p[..., 2 * cw + 3 * mw:2 * cw + 4 * mw]
    base = 2 * cw + 4 * mw
    ig = p[..., base:base + nh].astype(F32) + b_ig.astype(F32)
    lf = jax.nn.log_sigmoid(p[..., base + nh:base + 2 * nh].astype(F32) + b_fg.astype(F32))
    u = c_val * jax.nn.sigmoid(c_gate)
    ext = jnp.concatenate([conv_buf.astype(u.dtype), u], axis=1)
    y = lax.conv_general_dilated(ext, conv_w[:, None, :].astype(u.dtype), window_strides=(1,),
                                 padding='VALID', dimension_numbers=('NWC', 'WIO', 'NWC'),
                                 feature_group_count=cw) + conv_b.astype(u.dtype)
    y = jax.nn.silu(group_layer_norm(y, cn_g, cn_b))
    new_buf = ext[:, ext.shape[1] - (CONV_K - 1):]
    L = MLSTM_CHUNK if s_len % MLSTM_CHUNK == 0 else s_len
    nc = s_len // L

    def heads(t):
        return t.astype(F32).reshape(bsz, nc, L, nh, MLSTM_DH).transpose(1, 0, 3, 2, 4)

    def gates(t):
        return t.reshape(bsz, nc, L, nh).transpose(1, 0, 3, 2)

    carry0 = (c0.astype(F32), n0.astype(F32), m0.astype(F32))
    (c1, n1, m1), hs = lax.scan(mlstm_chunk, carry0,
                                (heads(q) * (MLSTM_DH ** -0.5), heads(k), heads(v), gates(ig), gates(lf)))
    hs = hs.transpose(1, 0, 3, 2, 4).reshape(bsz, s_len, nh, MLSTM_DH)
    hm = rms_norm(hs, mh_g.reshape(nh, MLSTM_DH)).reshape(bsz, s_len, mw).astype(h.dtype) * jax.nn.sigmoid(o)
    out = jnp.concatenate([y, hm], axis=-1) @ w_out
    return out, new_buf, c1, n1, m1


def fox_project(h, w_in, b_f, g_q, g_k):
    bsz, s_len, _ = h.shape
    p = h @ w_in
    shp = (bsz, s_len, FOX_HEADS, FOX_DH)
    q = rms_norm(p[..., :FOX_W].reshape(shp), g_q)
    k = rms_norm(p[..., FOX_W:2 * FOX_W].reshape(shp), g_k)
    v = p[..., 2 * FOX_W:3 * FOX_W].reshape(shp)
    lf = jax.nn.log_sigmoid(p[..., 3 * FOX_W:].astype(F32) + b_f.astype(F32))
    return q, k, v, lf


def fox_prompt(q, k, v, lf):
    bsz, s_len = q.shape[:2]
    scale = FOX_DH ** -0.5
    f_cum = jnp.cumsum(lf, axis=1).transpose(0, 2, 1)
    k_pos = jnp.arange(s_len)

    def block(j):
        start = j * Q_BLOCK
        qb = lax.dynamic_slice_in_dim(q, start, Q_BLOCK, axis=1)
        fq = lax.dynamic_slice_in_dim(f_cum, start, Q_BLOCK, axis=2)
        s = jnp.einsum('bqhd,bkhd->bhqk', qb, k).astype(F32) * scale
        s = s + fq[..., :, None] - f_cum[:, :, None, :]
        q_pos = start + jnp.arange(Q_BLOCK)
        s = jnp.where(q_pos[:, None] >= k_pos[None, :], s, -jnp.inf)
        pr = jax.nn.softmax(s, axis=-1).astype(v.dtype)
        return jnp.einsum('bhqk,bkhd->bqhd', pr, v)

    o = lax.map(block, jnp.arange(s_len // Q_BLOCK))
    return o.transpose(1, 0, 2, 3, 4).reshape(bsz, s_len, FOX_W)


def fox_sample(q, k, v, lf, cache_k, cache_v, cache_logf, page_table):
    bsz, t_len = q.shape[:2]
    past = page_table.shape[1] * cache_k.shape[1]
    scale = FOX_DH ** -0.5
    allowed = jnp.arange(past + t_len)[None, :] <= past + jnp.arange(t_len)[:, None]

    def one(args):
        qs, ks, vs, lfs, pages = args
        kk = jnp.concatenate([cache_k[pages].reshape(past, FOX_HEADS, FOX_DH), ks], axis=0)
        vv = jnp.concatenate([cache_v[pages].reshape(past, FOX_HEADS, FOX_DH), vs], axis=0)
        lfa = jnp.concatenate([cache_logf[pages].reshape(past, FOX_HEADS).astype(F32), lfs], axis=0)
        f_cum = jnp.cumsum(lfa, axis=0).T
        s = jnp.einsum('qhd,khd->hqk', qs, kk).astype(F32) * scale
        s = s + f_cum[:, past:, None] - f_cum[:, None, :]
        s = jnp.where(allowed[None], s, -jnp.inf)
        pr = jax.nn.softmax(s, axis=-1).astype(vv.dtype)
        return jnp.einsum('hqk,khd->qhd', pr, vv)

    o = lax.map(one, (q, k, v, lf, page_table))
    return o.reshape(bsz, t_len, FOX_W)


def peer(h, w_q, sub_keys, u_tab, v_tab):
    shp = h.shape
    x = h.reshape(-1, shp[-1])
    t = x.shape[0]
    q = (x @ w_q).reshape(t, PEER_HEADS, 2, PEER_DK // 2)
    s = jnp.einsum('thcd,hcnd->thcn', q, sub_keys).astype(F32)
    s1, i1 = lax.top_k(s[:, :, 0], PEER_TOPK)
    s2, i2 = lax.top_k(s[:, :, 1], PEER_TOPK)
    comb = (s1[..., :, None] + s2[..., None, :]).reshape(t, PEER_HEADS, PEER_TOPK * PEER_TOPK)
    sc, pos = lax.top_k(comb, PEER_TOPK)
    idx = (jnp.take_along_axis(i1, pos // PEER_TOPK, axis=-1) * PEER_NKEYS
           + jnp.take_along_axis(i2, pos % PEER_TOPK, axis=-1))
    gate = jax.nn.softmax(sc, axis=-1)
    n_sel = PEER_HEADS * PEER_TOPK
    idx = idx.reshape(t, n_sel)
    gate = gate.reshape(t, n_sel)
    nb = -(-t // PEER_TOKEN_BLOCK)
    pad = nb * PEER_TOKEN_BLOCK - t
    xb = jnp.pad(x, ((0, pad), (0, 0))).reshape(nb, PEER_TOKEN_BLOCK, shp[-1])
    ib = jnp.pad(idx, ((0, pad), (0, 0))).reshape(nb, PEER_TOKEN_BLOCK, n_sel)
    gb = jnp.pad(gate, ((0, pad), (0, 0))).reshape(nb, PEER_TOKEN_BLOCK, n_sel)

    def block(args):
        xt, it, gt = args
        act = jax.nn.gelu(jnp.einsum('tkd,td->tk', u_tab[it], xt).astype(F32), approximate=False) * gt
        return jnp.einsum('tk,tkd->td', act.astype(xt.dtype), v_tab[it])

    out = lax.map(block, (xb, ib, gb)).reshape(nb * PEER_TOKEN_BLOCK, shp[-1])[:t]
    return out.reshape(shp)


def setup_inputs(seed: int = 0) -> dict:
    key = jax.random.key(seed)
    ks = iter(jax.random.split(key, 64))

    def nrm(shape, scale):
        return jax.random.normal(next(ks), shape, F32) * scale

    d = D_MODEL
    n_pages = PAST_LEN // PAGE_SIZE
    n_used = DEC_BATCH * n_pages
    n_pool = n_used + n_used // 4 + 1
    page_table = jax.random.permutation(next(ks), n_pool)[:n_used].reshape(DEC_BATCH, n_pages).astype(jnp.int32)
    inp = {
        'x_prompt': nrm((BATCH, SEQ, d), 1.0),
        'x_sample': nrm((DEC_BATCH, DEC_SEQ, d), 1.0),
        'state_conv': nrm((DEC_BATCH, CONV_K - 1, CONV_W), 0.5),
        'state_C': nrm((DEC_BATCH, MLSTM_HEADS, MLSTM_DH, MLSTM_DH), 0.05),
        'state_n': nrm((DEC_BATCH, MLSTM_HEADS, MLSTM_DH), 0.1),
        'state_m': nrm((DEC_BATCH, MLSTM_HEADS), 1.0),
        'cache_k': nrm((n_pool, PAGE_SIZE, FOX_HEADS, FOX_DH), 1.0),
        'cache_v': nrm((n_pool, PAGE_SIZE, FOX_HEADS, FOX_DH), 1.0),
        'cache_logf': jax.nn.log_sigmoid(FOX_FORGET_BIAS + nrm((n_pool, PAGE_SIZE, FOX_HEADS), 1.0)),
        'page_table': page_table,
        'norm0_mix': 1.0 + nrm((d,), 0.02),
        'w_in0': nrm((d, IN0_COLS), d ** -0.5),
        'b_ig0': nrm((MLSTM_HEADS,), 0.1),
        'b_fg0': MLSTM_FORGET_BIAS + nrm((MLSTM_HEADS,), 0.5),
        'conv_w0': nrm((CONV_K, CONV_W), CONV_K ** -0.5),
        'conv_b0': nrm((CONV_W,), 0.01),
        'conv_norm_g0': 1.0 + nrm((CONV_W,), 0.02),
        'conv_norm_b0': nrm((CONV_W,), 0.01),
        'mlstm_norm_g0': 1.0 + nrm((MLSTM_W,), 0.02),
        'w_out0': nrm((CONV_W + MLSTM_W, d), (CONV_W + MLSTM_W) ** -0.5),
        'norm1_mix': 1.0 + nrm((d,), 0.02),
        'w_in1': nrm((d, IN1_COLS), d ** -0.5),
        'b_f1': FOX_FORGET_BIAS + nrm((FOX_HEADS,), 0.5),
        'q_norm_g1': 1.0 + nrm((FOX_DH,), 0.02),
        'k_norm_g1': 1.0 + nrm((FOX_DH,), 0.02),
        'w_out1': nrm((FOX_W, d), FOX_W ** -0.5),
        'norm0_ffn': 1.0 + nrm((d,), 0.02),
        'peer_wq0': nrm((d, PEER_HEADS * PEER_DK), d ** -0.5),
        'peer_keys0': nrm((PEER_HEADS, 2, PEER_NKEYS, PEER_DK // 2), (PEER_DK // 2) ** -0.5),
        'peer_u0': nrm((PEER_EXPERTS, d), d ** -0.5),
        'peer_v0': nrm((PEER_EXPERTS, d), PEER_HEADS ** -0.5),
        'norm1_ffn': 1.0 + nrm((d,), 0.02),
        'peer_wq1': nrm((d, PEER_HEADS * PEER_DK), d ** -0.5),
        'peer_keys1': nrm((PEER_HEADS, 2, PEER_NKEYS, PEER_DK // 2), (PEER_DK // 2) ** -0.5),
        'peer_u1': nrm((PEER_EXPERTS, d), d ** -0.5),
        'peer_v1': nrm((PEER_EXPERTS, d), PEER_HEADS ** -0.5),
    }
    return inp


def reference(x_prompt, x_sample, state_conv, state_C, state_n, state_m, cache_k, cache_v, cache_logf,
              page_table, norm0_mix, w_in0, b_ig0, b_fg0, conv_w0, conv_b0, conv_norm_g0, conv_norm_b0,
              mlstm_norm_g0, w_out0, norm1_mix, w_in1, b_f1, q_norm_g1, k_norm_g1, w_out1,
              norm0_ffn, peer_wq0, peer_keys0, peer_u0, peer_v0,
              norm1_ffn, peer_wq1, peer_keys1, peer_u1, peer_v1):
    mix0 = (w_in0, b_ig0, b_fg0, conv_w0, conv_b0, conv_norm_g0, conv_norm_b0, mlstm_norm_g0, w_out0)
    ffn = ((norm0_ffn, peer_wq0, peer_keys0, peer_u0, peer_v0),
           (norm1_ffn, peer_wq1, peer_keys1, peer_u1, peer_v1))
    bp = x_prompt.shape[0]
    xp, xs = x_prompt, x_sample
    for layer in range(DEPTH):
        if layer % 2 == 0:
            zero_buf = jnp.zeros((bp, CONV_K - 1, CONV_W), xp.dtype)
            zero_c = jnp.zeros((bp, MLSTM_HEADS, MLSTM_DH, MLSTM_DH), F32)
            zero_n = jnp.zeros((bp, MLSTM_HEADS, MLSTM_DH), F32)
            zero_m = jnp.zeros((bp, MLSTM_HEADS), F32)
            yp, p_conv, p_C, p_n, p_m = conv_mlstm_mixer(rms_norm(xp, norm0_mix), zero_buf, zero_c, zero_n, zero_m, *mix0)
            ys, s_conv, s_C, s_n, s_m = conv_mlstm_mixer(rms_norm(xs, norm0_mix), state_conv, state_C, state_n, state_m, *mix0)
        else:
            p_k, p_v, p_logf_q, p_logf = None, None, None, None
            qp, p_k, p_v, p_logf = fox_project(rms_norm(xp, norm1_mix), w_in1, b_f1, q_norm_g1, k_norm_g1)
            yp = fox_prompt(qp, p_k, p_v, p_logf) @ w_out1
            qs, s_k, s_v, s_logf = fox_project(rms_norm(xs, norm1_mix), w_in1, b_f1, q_norm_g1, k_norm_g1)
            ys = fox_sample(qs, s_k, s_v, s_logf, cache_k, cache_v, cache_logf, page_table) @ w_out1
        xp = xp + yp
        xs = xs + ys
        g_ffn, wq, keys, u_tab, v_tab = ffn[layer % 2]
        xp = xp + peer(rms_norm(xp, g_ffn), wq, keys, u_tab, v_tab)
        xs = xs + peer(rms_norm(xs, g_ffn), wq, keys, u_tab, v_tab)
    return (xp, xs, p_conv, p_C, p_n, p_m, p_k, p_v, p_logf, s_conv, s_C, s_n, s_m, s_k, s_v, s_logf)
```

```python
import functools

import jax
import jax.numpy as jnp
from jax import lax
from jax.experimental import pallas as pl
from jax.experimental.pallas import tpu as pltpu

F32 = jnp.float32
BF16 = jnp.bfloat16

D_MODEL = 2048
CONV_W = 1024
CONV_K = 31
CONV_GROUPS = 8
MLSTM_HEADS = 4
MLSTM_W = 1024
MLSTM_DH = 256
MLSTM_CHUNK = 128
FOX_HEADS = 16
FOX_DH = 128
FOX_W = 2048
Q_BLOCK = 128
PEER_HEADS = 8
PEER_NKEYS = 128
PEER_DK = 128
PEER_TOPK = 16
PEER_TOKEN_BLOCK = 256
RMS_EPS = 1e-6
LN_EPS = 1e-5


def _matmul_kernel(x_ref, w_ref, o_ref):
    o_ref[...] = jnp.dot(x_ref[...].astype(BF16), w_ref[...].astype(BF16), preferred_element_type=F32)


def matmul(x, w, tm=512, tn=512):
    m, k = x.shape
    n = w.shape[1]
    return pl.pallas_call(
        _matmul_kernel,
        grid=(m // tm, n // tn),
        in_specs=[pl.BlockSpec((tm, k), lambda i, j: (i, 0)), pl.BlockSpec((k, tn), lambda i, j: (0, j))],
        out_specs=pl.BlockSpec((tm, tn), lambda i, j: (i, j)),
        out_shape=jax.ShapeDtypeStruct((m, n), F32),
        name="matmul",
    )(x, w)


def rms_norm(x, g):
    xf = x.astype(F32)
    y = xf * lax.rsqrt(jnp.mean(jnp.square(xf), axis=-1, keepdims=True) + RMS_EPS)
    return (y * g.astype(F32)).astype(x.dtype)


def group_layer_norm(x, g, b):
    xf = x.astype(F32).reshape(x.shape[:-1] + (CONV_GROUPS, -1))
    mu = jnp.mean(xf, axis=-1, keepdims=True)
    var = jnp.mean(jnp.square(xf - mu), axis=-1, keepdims=True)
    y = ((xf - mu) * lax.rsqrt(var + LN_EPS)).reshape(x.shape)
    return (y * g.astype(F32) + b.astype(F32)).astype(x.dtype)


def mlstm_chunk(carry, inp):
    c0, n0, m0 = carry
    q, k, v, ig, lf = inp
    L = q.shape[2]
    f_cum = jnp.cumsum(lf, axis=-1)
    g = ig - f_cum
    m = f_cum + jnp.maximum(m0[..., None], lax.cummax(g, axis=2))
    a = jnp.exp(f_cum + m0[..., None] - m)
    causal = jnp.tril(jnp.ones((L, L), dtype=bool))
    log_d = (f_cum - m)[..., :, None] + g[..., None, :]
    d = jnp.exp(jnp.where(causal, log_d, -jnp.inf))
    qk = jnp.einsum('bhtd,bhsd->bhts', q, k) * d
    num = a[..., None] * jnp.einsum('bhtd,bhde->bhte', q, c0) + jnp.einsum('bhts,bhse->bhte', qk, v)
    den = a * jnp.einsum('bhtd,bhd->bht', q, n0) + jnp.sum(qk, axis=-1)
    h = num / jnp.maximum(jnp.abs(den), jnp.exp(-m))[..., None]
    d_last = d[..., -1, :]
    c1 = a[..., -1, None, None] * c0 + jnp.einsum('bhs,bhsd,bhse->bhde', d_last, k, v)
    n1 = a[..., -1, None] * n0 + jnp.einsum('bhs,bhsd->bhd', d_last, k)
    return (c1, n1, m[..., -1]), h


def conv_mlstm_mixer(h, conv_buf, c0, n0, m0, w_in, b_ig, b_fg, conv_w, conv_b, cn_g, cn_b, mh_g, w_out):
    bsz, s_len, _ = h.shape
    cw, mw, nh = CONV_W, MLSTM_W, MLSTM_HEADS
    p = h @ w_in
    c_val = p[..., :cw]
    c_gate = p[..., cw:2 * cw]
    q = p[..., 2 * cw:2 * cw + mw]
    k = p[..., 2 * cw + mw:2 * cw + 2 * mw]
    v = p[..., 2 * cw + 2 * mw:2 * cw + 3 * mw]
    o = p[..., 2 * cw + 3 * mw:2 * cw + 4 * mw]
    base = 2 * cw + 4 * mw
    ig = p[..., base:base + nh].astype(F32) + b_ig.astype(F32)
    lf = jax.nn.log_sigmoid(p[..., base + nh:base + 2 * nh].astype(F32) + b_fg.astype(F32))
    u = c_val * jax.nn.sigmoid(c_gate)
    ext = jnp.concatenate([conv_buf.astype(u.dtype), u], axis=1)
    y = lax.conv_general_dilated(ext, conv_w[:, None, :].astype(u.dtype), window_strides=(1,),
                                 padding='VALID', dimension_numbers=('NWC', 'WIO', 'NWC'),
                                 feature_group_count=cw) + conv_b.astype(u.dtype)
    y = jax.nn.silu(group_layer_norm(y, cn_g, cn_b))
    new_buf = ext[:, ext.shape[1] - (CONV_K - 1):]
    L = MLSTM_CHUNK if s_len % MLSTM_CHUNK == 0 else s_len
    nc = s_len // L

    def heads(t):
        return t.astype(F32).reshape(bsz, nc, L, nh, MLSTM_DH).transpose(1, 0, 3, 2, 4)

    def gates(t):
        return t.reshape(bsz, nc, L, nh).transpose(1, 0, 3, 2)

    carry0 = (c0.astype(F32), n0.astype(F32), m0.astype(F32))
    (c1, n1, m1), hs = lax.scan(mlstm_chunk, carry0,
                                (heads(q) * (MLSTM_DH ** -0.5), heads(k), heads(v), gates(ig), gates(lf)))
    hs = hs.transpose(1, 0, 3, 2, 4).reshape(bsz, s_len, nh, MLSTM_DH)
    hm = rms_norm(hs, mh_g.reshape(nh, MLSTM_DH)).reshape(bsz, s_len, mw).astype(h.dtype) * jax.nn.sigmoid(o)
    cat = jnp.concatenate([y, hm], axis=-1)
    out = matmul(cat.reshape(bsz * s_len, -1), w_out).reshape(bsz, s_len, -1)
    return out, new_buf, c1, n1, m1


def fox_project(h, w_in, b_f, g_q, g_k):
    bsz, s_len, _ = h.shape
    p = h @ w_in
    shp = (bsz, s_len, FOX_HEADS, FOX_DH)
    q = rms_norm(p[..., :FOX_W].reshape(shp), g_q)
    k = rms_norm(p[..., FOX_W:2 * FOX_W].reshape(shp), g_k)
    v = p[..., 2 * FOX_W:3 * FOX_W].reshape(shp)
    lf = jax.nn.log_sigmoid(p[..., 3 * FOX_W:].astype(F32) + b_f.astype(F32))
    return q, k, v, lf


def fox_prompt(q, k, v, lf):
    bsz, s_len = q.shape[:2]
    scale = FOX_DH ** -0.5
    f_cum = jnp.cumsum(lf, axis=1).transpose(0, 2, 1)
    k_pos = jnp.arange(s_len)

    def block(j):
        start = j * Q_BLOCK
        qb = lax.dynamic_slice_in_dim(q, start, Q_BLOCK, axis=1)
        fq = lax.dynamic_slice_in_dim(f_cum, start, Q_BLOCK, axis=2)
        s = jnp.einsum('bqhd,bkhd->bhqk', qb, k).astype(F32) * scale
        s = s + fq[..., :, None] - f_cum[:, :, None, :]
        q_pos = start + jnp.arange(Q_BLOCK)
        s = jnp.where(q_pos[:, None] >= k_pos[None, :], s, -jnp.inf)
        pr = jax.nn.softmax(s, axis=-1).astype(v.dtype)
        return jnp.einsum('bhqk,bkhd->bqhd', pr, v)

    o = lax.map(block, jnp.arange(s_len // Q_BLOCK))
    return o.transpose(1, 0, 2, 3, 4).reshape(bsz, s_len, FOX_W)


def fox_sample(q, k, v, lf, cache_k, cache_v, cache_logf, page_table):
    bsz, t_len = q.shape[:2]
    past = page_table.shape[1] * cache_k.shape[1]
    scale = FOX_DH ** -0.5
    allowed = jnp.arange(past + t_len)[None, :] <= past + jnp.arange(t_len)[:, None]

    def one(args):
        qs, ks, vs, lfs, pages = args
        kk = jnp.concatenate([cache_k[pages].reshape(past, FOX_HEADS, FOX_DH), ks], axis=0)
        vv = jnp.concatenate([cache_v[pages].reshape(past, FOX_HEADS, FOX_DH), vs], axis=0)
        lfa = jnp.concatenate([cache_logf[pages].reshape(past, FOX_HEADS).astype(F32), lfs], axis=0)
        f_cum = jnp.cumsum(lfa, axis=0).T
        s = jnp.einsum('qhd,khd->hqk', qs, kk).astype(F32) * scale
        s = s + f_cum[:, past:, None] - f_cum[:, None, :]
        s = jnp.where(allowed[None], s, -jnp.inf)
        pr = jax.nn.softmax(s, axis=-1).astype(vv.dtype)
        return jnp.einsum('hqk,khd->qhd', pr, vv)

    o = lax.map(one, (q, k, v, lf, page_table))
    return o.reshape(bsz, t_len, FOX_W)


def peer(h, w_q, sub_keys, u_tab, v_tab):
    shp = h.shape
    x = h.reshape(-1, shp[-1])
    t = x.shape[0]
    q = (x @ w_q).reshape(t, PEER_HEADS, 2, PEER_DK // 2)
    s = jnp.einsum('thcd,hcnd->thcn', q, sub_keys).astype(F32)
    s1, i1 = lax.top_k(s[:, :, 0], PEER_TOPK)
    s2, i2 = lax.top_k(s[:, :, 1], PEER_TOPK)
    comb = (s1[..., :, None] + s2[..., None, :]).reshape(t, PEER_HEADS, PEER_TOPK * PEER_TOPK)
    sc, pos = lax.top_k(comb, PEER_TOPK)
    idx = (jnp.take_along_axis(i1, pos // PEER_TOPK, axis=-1) * PEER_NKEYS
           + jnp.take_along_axis(i2, pos % PEER_TOPK, axis=-1))
    gate = jax.nn.softmax(sc, axis=-1)
    n_sel = PEER_HEADS * PEER_TOPK
    idx = idx.reshape(t, n_sel)
    gate = gate.reshape(t, n_sel)
    nb = -(-t // PEER_TOKEN_BLOCK)
    pad = nb * PEER_TOKEN_BLOCK - t
    xb = jnp.pad(x, ((0, pad), (0, 0))).reshape(nb, PEER_TOKEN_BLOCK, shp[-1])
    ib = jnp.pad(idx, ((0, pad), (0, 0))).reshape(nb, PEER_TOKEN_BLOCK, n_sel)
    gb = jnp.pad(gate, ((0, pad), (0, 0))).reshape(nb, PEER_TOKEN_BLOCK, n_sel)

    def block(args):
        xt, it, gt = args
        act = jax.nn.gelu(jnp.einsum('tkd,td->tk', u_tab[it], xt).astype(F32), approximate=False) * gt
        return jnp.einsum('tk,tkd->td', act.astype(xt.dtype), v_tab[it])

    out = lax.map(block, (xb, ib, gb)).reshape(nb * PEER_TOKEN_BLOCK, shp[-1])[:t]
    return out.reshape(shp)


def kernel(x_prompt, x_sample, state_conv, state_C, state_n, state_m, cache_k, cache_v, cache_logf, page_table,
           norm0_mix, w_in0, b_ig0, b_fg0, conv_w0, conv_b0, conv_norm_g0, conv_norm_b0, mlstm_norm_g0, w_out0,
           norm1_mix, w_in1, b_f1, q_norm_g1, k_norm_g1, w_out1,
           norm0_ffn, peer_wq0, peer_keys0, peer_u0, peer_v0,
           norm1_ffn, peer_wq1, peer_keys1, peer_u1, peer_v1):
    mix0 = (w_in0, b_ig0, b_fg0, conv_w0, conv_b0, conv_norm_g0, conv_norm_b0, mlstm_norm_g0, w_out0)
    bp = x_prompt.shape[0]
    xp, xs = x_prompt, x_sample
    zero_buf = jnp.zeros((bp, CONV_K - 1, CONV_W), xp.dtype)
    zero_c = jnp.zeros((bp, MLSTM_HEADS, MLSTM_DH, MLSTM_DH), F32)
    zero_n = jnp.zeros((bp, MLSTM_HEADS, MLSTM_DH), F32)
    zero_m = jnp.zeros((bp, MLSTM_HEADS), F32)
    yp, p_conv, p_C, p_n, p_m = conv_mlstm_mixer(rms_norm(xp, norm0_mix), zero_buf, zero_c, zero_n, zero_m, *mix0)
    ys, s_conv, s_C, s_n, s_m = conv_mlstm_mixer(rms_norm(xs, norm0_mix), state_conv, state_C, state_n, state_m, *mix0)
    xp = xp + yp
    xs = xs + ys
    xp = xp + peer(rms_norm(xp, norm0_ffn), peer_wq0, peer_keys0, peer_u0, peer_v0)
    xs = xs + peer(rms_norm(xs, norm0_ffn), peer_wq0, peer_keys0, peer_u0, peer_v0)
    qp, p_k, p_v, p_logf = fox_project(rms_norm(xp, norm1_mix), w_in1, b_f1, q_norm_g1, k_norm_g1)
    ap = fox_prompt(qp, p_k, p_v, p_logf)
    yp = matmul(ap.reshape(-1, FOX_W), w_out1).reshape(xp.shape)
    qs, s_k, s_v, s_logf = fox_project(rms_norm(xs, norm1_mix), w_in1, b_f1, q_norm_g1, k_norm_g1)
    a_s = fox_sample(qs, s_k, s_v, s_logf, cache_k, cache_v, cache_logf, page_table)
    ys = matmul(a_s.reshape(-1, FOX_W), w_out1).reshape(xs.shape)
    xp = xp + yp
    xs = xs + ys
    xp = xp + peer(rms_norm(xp, norm1_ffn), peer_wq1, peer_keys1, peer_u1, peer_v1)
    xs = xs + peer(rms_norm(xs, norm1_ffn), peer_wq1, peer_keys1, peer_u1, peer_v1)
    return (xp, xs, p_conv, p_C, p_n, p_m, p_k, p_v, p_logf, s_conv, s_C, s_n, s_m, s_k, s_v, s_logf)
```

```python
import functools
import math

import jax
import jax.numpy as jnp
import numpy as np
from jax import lax
from jax.experimental import pallas as pl
from jax.experimental.pallas import tpu as pltpu

F32 = jnp.float32
BF16 = jnp.bfloat16

D_MODEL = 2048
CONV_W = 1024
CONV_K = 31
CONV_GROUPS = 8
MLSTM_HEADS = 4
MLSTM_W = 1024
MLSTM_DH = 256
MLSTM_CHUNK = 128
FOX_HEADS = 16
FOX_DH = 128
FOX_W = 2048
PEER_HEADS = 8
PEER_NKEYS = 128
PEER_DK = 128
PEER_TOPK = 16
RMS_EPS = 1e-6
LN_EPS = 1e-5

LANES = 128
VMEM_LIMIT = 56 * 1024 * 1024
NEG_INF = float("-inf")


def _params(*sem):
    return pltpu.CompilerParams(dimension_semantics=sem, vmem_limit_bytes=VMEM_LIMIT)


def _rmsnorm_kernel(x_ref, g_ref, o_ref):
    x = x_ref[...]
    y = x * lax.rsqrt(jnp.mean(x * x, axis=-1, keepdims=True) + RMS_EPS)
    o_ref[...] = (y * g_ref[...]).astype(o_ref.dtype)


def rmsnorm_bf16(x, g):
    t, d = x.shape
    tm = _pick(t, (512, 256, 128))
    return pl.pallas_call(
        _rmsnorm_kernel,
        grid=(t // tm,),
        in_specs=[pl.BlockSpec((tm, d), lambda i: (i, 0)), pl.BlockSpec((1, d), lambda i: (0, 0))],
        out_specs=pl.BlockSpec((tm, d), lambda i: (i, 0)),
        out_shape=jax.ShapeDtypeStruct((t, d), BF16),
        compiler_params=_params("parallel"),
        name="rmsnorm",
    )(x, g.reshape(1, d))


def _gate_proj_kernel(x_ref, g_ref, w_ref, o_ref):
    x = x_ref[...]
    y = x * lax.rsqrt(jnp.mean(x * x, axis=-1, keepdims=True) + RMS_EPS) * g_ref[...]
    o_ref[...] = jnp.dot(y, w_ref[...], precision=lax.Precision.HIGHEST, preferred_element_type=F32)


def gate_proj(x, g, w):
    t, d = x.shape
    n = w.shape[1]
    tm = _pick(t, (256, 128))
    out = pl.pallas_call(
        _gate_proj_kernel,
        grid=(t // tm,),
        in_specs=[pl.BlockSpec((tm, d), lambda i: (i, 0)), pl.BlockSpec((1, d), lambda i: (0, 0)),
                  pl.BlockSpec((d, LANES), lambda i: (0, 0))],
        out_specs=pl.BlockSpec((tm, LANES), lambda i: (i, 0)),
        out_shape=jax.ShapeDtypeStruct((t, LANES), F32),
        compiler_params=_params("parallel"),
        name="gate_proj",
    )(x, g.reshape(1, d), jnp.pad(w, ((0, 0), (0, LANES - n))))
    return out[:, :n]


def _mm_kernel(x_ref, w_ref, o_ref):
    o_ref[...] = jnp.dot(x_ref[...], w_ref[...], preferred_element_type=F32).astype(o_ref.dtype)


def mm(x, w, tm, tn, out_dtype=F32):
    m, k = x.shape
    n = w.shape[1]
    assert m % tm == 0 and n % tn == 0, (m, n, tm, tn)
    return pl.pallas_call(
        _mm_kernel,
        grid=(n // tn, m // tm),
        in_specs=[pl.BlockSpec((tm, k), lambda j, i: (i, 0)), pl.BlockSpec((k, tn), lambda j, i: (0, j))],
        out_specs=pl.BlockSpec((tm, tn), lambda j, i: (i, j)),
        out_shape=jax.ShapeDtypeStruct((m, n), out_dtype),
        compiler_params=_params("parallel", "parallel"),
        name="mm",
    )(x, w)


def _pick(n, cands):
    for c in cands:
        if n % c == 0:
            return c
    raise ValueError(n)


def mm_tokens(x, w, out_dtype=F32):
    t = x.shape[0]
    n = w.shape[1]
    return mm(x, w, _pick(t, (512, 256, 128)), _pick(n, (1024, 512, 256, 128)), out_dtype)


def _topk_rows(s, k):
    n = s.shape[0]
    iota = lax.broadcasted_iota(jnp.int32, s.shape, 0)
    iota_k = lax.broadcasted_iota(jnp.int32, (k, s.shape[1]), 0)

    def body(r, carry):
        w, rank, vals = carry
        mx = jnp.max(w, axis=0, keepdims=True)
        idx = jnp.min(jnp.where(w == mx, iota, n), axis=0, keepdims=True)
        hit = iota == idx
        w = jnp.where(hit, NEG_INF, w)
        rank = jnp.where(hit, r, rank)
        vals = jnp.where(iota_k == r, mx, vals)
        return w, rank, vals

    init = (s, jnp.full(s.shape, k, jnp.int32), jnp.zeros((k, s.shape[1]), F32))
    _, rank, vals = lax.fori_loop(0, k, body, init)
    return vals, rank


def _peer_select_kernel(s_ref, rank2_ref, e2z_ref, nsel_ref, e1_ref):
    k = PEER_TOPK
    half = k // 2
    L = s_ref.shape[1]
    sub = lax.broadcasted_iota(jnp.int32, (half, L), 0)

    def head(h, _):
        base = pl.multiple_of(h * (2 * PEER_NKEYS), 2 * PEER_NKEYS)
        s1 = s_ref[pl.ds(base, PEER_NKEYS), :]
        s2 = s_ref[pl.ds(base + PEER_NKEYS, PEER_NKEYS), :]
        v1, rank1 = _topk_rows(s1, k)
        v2, rank2 = _topk_rows(s2, k)
        pieces, pos = [], []
        for r1 in range(half):
            pieces.append(v1[r1:r1 + 1, :] + v2[0:half, :])
            pos.append(sub + r1 * k)
        pieces.append(v1[0:1, :] + v2[half:k, :])
        pos.append(sub + half)
        pieces.append(v1[half:k, :] + v2[0:1, :])
        pos.append((sub + half) * k)
        cand = jnp.concatenate(pieces, axis=0)
        posid = jnp.concatenate(pos, axis=0)
        big = k * k

        def pick(_, carry):
            w, sel = carry
            mx = jnp.max(w, axis=0, keepdims=True)
            p = jnp.min(jnp.where(w == mx, posid, big), axis=0, keepdims=True)
            hit = posid == p
            return jnp.where(hit, NEG_INF, w), jnp.where(hit, 1.0, sel)

        _, sel = lax.fori_loop(0, k, pick, (cand, jnp.zeros(cand.shape, F32)))
        top = v1[0:1, :] + v2[0:1, :]
        z = jnp.sum(sel * jnp.exp(cand - top), axis=0, keepdims=True)
        counts = []
        for r1 in range(half):
            c = jnp.sum(sel[r1 * half:(r1 + 1) * half, :], axis=0, keepdims=True)
            if r1 == 0:
                c = c + jnp.sum(sel[half * half:half * half + half, :], axis=0, keepdims=True)
            counts.append(c)
        tail = sel[half * half + half:, :]
        for r1 in range(half, k):
            counts.append(tail[r1 - half:r1 - half + 1, :])
        nsel = jnp.zeros(s1.shape, F32)
        for r1 in range(k):
            nsel = jnp.where(rank1 == r1, counts[r1], nsel)
        rank2_ref[h] = rank2.astype(F32)
        nsel_ref[h] = nsel
        e1_ref[h] = jnp.exp(s1 - v1[0:1, :])
        e2z_ref[h] = jnp.exp(s2 - v2[0:1, :]) / z
        return 0

    lax.fori_loop(0, PEER_HEADS, head, 0)


def peer_select(s_t):
    rows, t = s_t.shape
    spec = pl.BlockSpec((PEER_HEADS, PEER_NKEYS, LANES), lambda i: (0, 0, i))
    shp = jax.ShapeDtypeStruct((PEER_HEADS, PEER_NKEYS, t), F32)
    return pl.pallas_call(
        _peer_select_kernel,
        grid=(t // LANES,),
        in_specs=[pl.BlockSpec((rows, LANES), lambda i: (0, i))],
        out_specs=[spec] * 4,
        out_shape=[shp] * 4,
        compiler_params=_params("parallel"),
        name="peer_select",
    )(s_t)


def _gelu(x):
    return 0.5 * x * (1.0 + lax.erf(x * (1.0 / math.sqrt(2.0))))


def _peer_dense_kernel(xn_ref, u_ref, vt_ref, rank2_ref, e2z_ref, nsel_ref, e1_ref, o_ref, a_ref):
    j = pl.program_id(1)
    nb = u_ref.shape[0] // PEER_NKEYS

    @pl.when(j == 0)
    def _():
        o_ref[...] = jnp.zeros_like(o_ref)

    for kb in range(nb):
        a = j * nb + kb
        s = jnp.dot(u_ref[kb * PEER_NKEYS:(kb + 1) * PEER_NKEYS, :], xn_ref[...], preferred_element_type=F32)
        g = jnp.zeros(s.shape, F32)
        for h in range(PEER_HEADS):
            n_row = nsel_ref[h, pl.ds(a, 1), :]
            e_row = e1_ref[h, pl.ds(a, 1), :]
            g = g + jnp.where(rank2_ref[h] < n_row, e2z_ref[h], 0.0) * e_row
        a_ref[kb * PEER_NKEYS:(kb + 1) * PEER_NKEYS, :] = (_gelu(s) * g).astype(BF16)
    o_ref[...] += jnp.dot(vt_ref[...], a_ref[...], preferred_element_type=F32)


def peer_dense(xn_t, u_bf, vt_bf, rank2, e2z, nsel, e1, tm=512, te=512):
    d, t = xn_t.shape
    n_exp = u_bf.shape[0]
    sel_spec = pl.BlockSpec((PEER_HEADS, PEER_NKEYS, tm), lambda i, j: (0, 0, i))
    return pl.pallas_call(
        _peer_dense_kernel,
        grid=(t // tm, n_exp // te),
        in_specs=[pl.BlockSpec((d, tm), lambda i, j: (0, i)),
                  pl.BlockSpec((te, d), lambda i, j: (j, 0)),
                  pl.BlockSpec((d, te), lambda i, j: (0, j)),
                  sel_spec, sel_spec, sel_spec, sel_spec],
        out_specs=pl.BlockSpec((d, tm), lambda i, j: (0, i)),
        out_shape=jax.ShapeDtypeStruct((d, t), F32),
        scratch_shapes=[pltpu.VMEM((te, tm), BF16)],
        compiler_params=_params("parallel", "arbitrary"),
        name="peer_dense",
    )(xn_t, u_bf, vt_bf, rank2, e2z, nsel, e1)


def peer(x, g, w_q, sub_keys, u_tab, v_tab):
    t, d = x.shape
    xn = rmsnorm_bf16(x, g)
    xn_t = xn.T
    wq_t = w_q.T.astype(BF16)
    q_t = mm(wq_t, xn_t, _pick(wq_t.shape[0], (512, 256, 128)), _pick(t, (512, 256, 128)), BF16)
    hc = PEER_HEADS * 2
    dk2 = PEER_DK // 2
    keys = sub_keys.reshape(hc, PEER_NKEYS, dk2).astype(BF16)
    eye = jnp.eye(hc, dtype=BF16)
    kbd_t = (keys[:, :, None, :] * eye[:, None, :, None]).reshape(hc * PEER_NKEYS, hc * dk2)
    s_t = mm(kbd_t, q_t, _pick(kbd_t.shape[0], (512, 256, 128)), _pick(t, (512, 256, 128)))
    rank2, e2z, nsel, e1 = peer_select(s_t)
    out_t = peer_dense(xn_t, u_tab.astype(BF16), v_tab.T.astype(BF16), rank2, e2z, nsel, e1,
                       tm=_pick(t, (512, 256, 128)))
    return out_t.T


def _fox_norm_kernel(p_ref, gq_ref, gk_ref, q_ref, k_ref):
    scale = FOX_DH ** -0.5
    for h in range(FOX_HEADS):
        lo, hi = h * FOX_DH, (h + 1) * FOX_DH
        x = p_ref[:, lo:hi]
        y = x * lax.rsqrt(jnp.mean(x * x, axis=-1, keepdims=True) + RMS_EPS) * gq_ref[...]
        q_ref[:, lo:hi] = (y * scale).astype(q_ref.dtype)
        x = p_ref[:, FOX_W + lo:FOX_W + hi]
        k_ref[:, lo:hi] = x * lax.rsqrt(jnp.mean(x * x, axis=-1, keepdims=True) + RMS_EPS) * gk_ref[...]


def fox_norm(p, g_q, g_k):
    t = p.shape[0]
    tm = _pick(t, (256, 128))
    return pl.pallas_call(
        _fox_norm_kernel,
        grid=(t // tm,),
        in_specs=[pl.BlockSpec((tm, 2 * FOX_W), lambda i: (i, 0)),
                  pl.BlockSpec((1, FOX_DH), lambda i: (0, 0)), pl.BlockSpec((1, FOX_DH), lambda i: (0, 0))],
        out_specs=[pl.BlockSpec((tm, FOX_W), lambda i: (i, 0))] * 2,
        out_shape=[jax.ShapeDtypeStruct((t, FOX_W), BF16), jax.ShapeDtypeStruct((t, FOX_W), F32)],
        compiler_params=_params("parallel"),
        name="fox_norm",
    )(p, g_q.reshape(1, FOX_DH), g_k.reshape(1, FOX_DH))


def _fox_prompt_kernel(qi_ref, ki_ref, q_ref, k_ref, v_ref, f_ref, o_ref, m_ref, l_ref, acc_ref, *, tq, tk):
    n = pl.program_id(2)
    i = qi_ref[n]
    kk = ki_ref[n]

    @pl.when(kk == 0)
    def _():
        m_ref[...] = jnp.full_like(m_ref, NEG_INF)
        l_ref[...] = jnp.zeros_like(l_ref)
        acc_ref[...] = jnp.zeros_like(acc_ref)

    def step(masked):
        s = lax.dot_general(q_ref[...], k_ref[...].astype(BF16), (((1,), (1,)), ((), ())),
                            preferred_element_type=F32)
        s = s - f_ref[0]
        if masked:
            q_pos = i * tq + lax.broadcasted_iota(jnp.int32, s.shape, 0)
            k_pos = kk * tk + lax.broadcasted_iota(jnp.int32, s.shape, 1)
            s = jnp.where(q_pos >= k_pos, s, NEG_INF)
        m_prev = m_ref[...]
        m_new = jnp.maximum(m_prev, jnp.max(s, axis=1, keepdims=True))
        alpha = jnp.exp(m_prev - m_new)
        p = jnp.exp(s - m_new)
        l_ref[...] = alpha * l_ref[...] + jnp.sum(p, axis=1, keepdims=True)
        acc_ref[...] = alpha * acc_ref[...] + jnp.dot(p.astype(BF16), v_ref[...].astype(BF16),
                                                      preferred_element_type=F32)
        m_ref[...] = m_new

    on_diag = (kk + 1) * tk > i * tq + 1

    @pl.when(on_diag)
    def _():
        step(True)

    @pl.when(jnp.logical_not(on_diag))
    def _():
        step(False)

    @pl.when(kk == (i * tq + tq - 1) // tk)
    def _():
        o_ref[...] = (acc_ref[...] / l_ref[...]).astype(o_ref.dtype)


def fox_prompt(q_bf, k, p, f_cum, bsz, s_len, tq=1024, tk=512):
    nq, nk = s_len // tq, s_len // tk
    pairs = [(i, kk) for i in range(nq) for kk in range((i * tq + tq - 1) // tk + 1)]
    qi = jnp.asarray([a for a, _ in pairs], jnp.int32)
    ki = jnp.asarray([b for _, b in pairs], jnp.int32)
    v_col0 = 2 * FOX_W // FOX_DH
    grid_spec = pltpu.PrefetchScalarGridSpec(
        num_scalar_prefetch=2,
        grid=(bsz, FOX_HEADS, len(pairs)),
        in_specs=[pl.BlockSpec((tq, FOX_DH), lambda b, h, n, qi, ki: (b * nq + qi[n], h)),
                  pl.BlockSpec((tk, FOX_DH), lambda b, h, n, qi, ki: (b * nk + ki[n], h)),
                  pl.BlockSpec((tk, FOX_DH), lambda b, h, n, qi, ki: (b * nk + ki[n], v_col0 + h)),
                  pl.BlockSpec((1, 1, tk), lambda b, h, n, qi, ki: (b * FOX_HEADS + h, 0, ki[n]))],
        out_specs=pl.BlockSpec((tq, FOX_DH), lambda b, h, n, qi, ki: (b * nq + qi[n], h)),
        scratch_shapes=[pltpu.VMEM((tq, 1), F32), pltpu.VMEM((tq, 1), F32), pltpu.VMEM((tq, FOX_DH), F32)],
    )
    return pl.pallas_call(
        functools.partial(_fox_prompt_kernel, tq=tq, tk=tk),
        grid_spec=grid_spec,
        out_shape=jax.ShapeDtypeStruct((bsz * s_len, FOX_W), BF16),
        compiler_params=_params("parallel", "parallel", "arbitrary"),
        name="fox_prompt",
    )(qi, ki, q_bf, k, p, f_cum)


SUB = 8


def _fox_sample_kernel(pt_ref, qbd_ref, kc_ref, vc_ref, lfc_ref, kn_ref, vn_ref, lfn_ref, o_ref,
                       m_ref, l_ref, acc_ref, fcar_ref, kpad_ref, vpad_ref, lfpad_ref, *, n_pages, t_len):
    pg = pl.program_id(1)
    page = kc_ref.shape[0]
    rows = FOX_HEADS * SUB
    hi = lax.Precision.HIGHEST

    @pl.when(pg == 0)
    def _():
        m_ref[...] = jnp.full_like(m_ref, NEG_INF)
        l_ref[...] = jnp.zeros_like(l_ref)
        acc_ref[...] = jnp.zeros_like(acc_ref)
        fcar_ref[...] = jnp.zeros_like(fcar_ref)

    expand = (lax.broadcasted_iota(jnp.int32, (rows, FOX_HEADS), 0) // SUB
              == lax.broadcasted_iota(jnp.int32, (rows, FOX_HEADS), 1)).astype(F32)
    tri = (lax.broadcasted_iota(jnp.int32, (page, page), 0)
           <= lax.broadcasted_iota(jnp.int32, (page, page), 1)).astype(F32)

    def attend(kb, vb, lf, mask):
        s = lax.dot_general(qbd_ref[0], kb, (((1,), (1,)), ((), ())), preferred_element_type=F32)
        lf_rows = lax.dot_general(expand, lf, (((1,), (1,)), ((), ())), precision=hi, preferred_element_type=F32)
        f = jnp.dot(lf_rows, tri, precision=hi, preferred_element_type=F32) + fcar_ref[...]
        fcar_ref[...] = f[:, page - 1:page]
        s = s - f
        if mask is not None:
            s = jnp.where(mask, s, NEG_INF)
        m_prev = m_ref[...]
        m_new = jnp.maximum(m_prev, jnp.max(s, axis=1, keepdims=True))
        alpha = jnp.exp(m_prev - m_new)
        p = jnp.exp(s - m_new)
        l_ref[...] = alpha * l_ref[...] + jnp.sum(p, axis=1, keepdims=True)
        acc_ref[...] = alpha * acc_ref[...] + jnp.dot(p.astype(BF16), vb, preferred_element_type=F32)
        m_ref[...] = m_new

    @pl.when(pg < n_pages)
    def _():
        attend(kc_ref[...].astype(BF16), vc_ref[...].astype(BF16), lfc_ref[0], None)

    @pl.when(pg == n_pages)
    def _():
        kpad_ref[...] = jnp.zeros_like(kpad_ref)
        vpad_ref[...] = jnp.zeros_like(vpad_ref)
        lfpad_ref[...] = jnp.zeros_like(lfpad_ref)
        kpad_ref[0:SUB, :] = kn_ref[0].astype(BF16)
        vpad_ref[0:SUB, :] = vn_ref[0].astype(BF16)
        lfpad_ref[0:SUB, :] = lfn_ref[0]
        q_idx = lax.broadcasted_iota(jnp.int32, (rows, page), 0) % SUB
        k_idx = lax.broadcasted_iota(jnp.int32, (rows, page), 1)
        mask = k_idx <= jnp.minimum(q_idx, t_len - 1)
        attend(kpad_ref[...], vpad_ref[...], lfpad_ref[...], mask)
        inv = 1.0 / l_ref[...]
        for h in range(FOX_HEADS):
            o_ref[0, :, h * FOX_DH:(h + 1) * FOX_DH] = (
                acc_ref[h * SUB:(h + 1) * SUB, h * FOX_DH:(h + 1) * FOX_DH] * inv[h * SUB:(h + 1) * SUB, :])


def fox_sample(q_bf, k_new, v_new, lf_new, cache_k, cache_v, cache_logf, page_table):
    bsz, t_len, _ = q_bf.shape
    n_pool, page = cache_k.shape[:2]
    n_pages = page_table.shape[1]
    pad = ((0, 0), (0, SUB - t_len), (0, 0))
    q4 = jnp.pad(q_bf, pad).reshape(bsz, SUB, FOX_HEADS, FOX_DH)
    eye = jnp.eye(FOX_HEADS, dtype=BF16)
    qbd = (q4.transpose(0, 2, 1, 3)[:, :, :, None, :] * eye[None, :, None, :, None]).reshape(
        bsz, FOX_HEADS * SUB, FOX_W)
    kc = cache_k.reshape(n_pool * page, FOX_W)
    vc = cache_v.reshape(n_pool * page, FOX_W)
    last = n_pages - 1
    rows = FOX_HEADS * SUB
    grid_spec = pltpu.PrefetchScalarGridSpec(
        num_scalar_prefetch=1,
        grid=(bsz, n_pages + 1),
        in_specs=[pl.BlockSpec((1, rows, FOX_W), lambda b, p, pt: (b, 0, 0)),
                  pl.BlockSpec((page, FOX_W), lambda b, p, pt: (pt[b, jnp.minimum(p, last)], 0)),
                  pl.BlockSpec((page, FOX_W), lambda b, p, pt: (pt[b, jnp.minimum(p, last)], 0)),
                  pl.BlockSpec((1, page, FOX_HEADS), lambda b, p, pt: (pt[b, jnp.minimum(p, last)], 0, 0)),
                  pl.BlockSpec((1, SUB, FOX_W), lambda b, p, pt: (b, 0, 0)),
                  pl.BlockSpec((1, SUB, FOX_W), lambda b, p, pt: (b, 0, 0)),
                  pl.BlockSpec((1, SUB, FOX_HEADS), lambda b, p, pt: (b, 0, 0))],
        out_specs=pl.BlockSpec((1, SUB, FOX_W), lambda b, p, pt: (b, 0, 0)),
        scratch_shapes=[pltpu.VMEM((rows, 1), F32), pltpu.VMEM((rows, 1), F32), pltpu.VMEM((rows, FOX_W), F32),
                        pltpu.VMEM((rows, 1), F32), pltpu.VMEM((page, FOX_W), BF16), pltpu.VMEM((page, FOX_W), BF16),
                        pltpu.VMEM((page, FOX_HEADS), F32)],
    )
    out = pl.pallas_call(
        functools.partial(_fox_sample_kernel, n_pages=n_pages, t_len=t_len),
        grid_spec=grid_spec,
        out_shape=jax.ShapeDtypeStruct((bsz, SUB, FOX_W), F32),
        compiler_params=_params("parallel", "arbitrary"),
        name="fox_sample",
    )(page_table, qbd, kc, vc, cache_logf, jnp.pad(k_new, pad), jnp.pad(v_new, pad), jnp.pad(lf_new, pad))
    return out[:, :t_len]


def rms_norm(x, g):
    xf = x.astype(F32)
    y = xf * lax.rsqrt(jnp.mean(jnp.square(xf), axis=-1, keepdims=True) + RMS_EPS)
    return (y * g.astype(F32)).astype(x.dtype)


def group_layer_norm(x, g, b):
    xf = x.astype(F32).reshape(x.shape[:-1] + (CONV_GROUPS, -1))
    mu = jnp.mean(xf, axis=-1, keepdims=True)
    var = jnp.mean(jnp.square(xf - mu), axis=-1, keepdims=True)
    y = ((xf - mu) * lax.rsqrt(var + LN_EPS)).reshape(x.shape)
    return (y * g.astype(F32) + b.astype(F32)).astype(x.dtype)


def mlstm_chunk(carry, inp):
    c0, n0, m0 = carry
    q, k, v, ig, lf = inp
    L = q.shape[2]
    f_cum = jnp.cumsum(lf, axis=-1)
    g = ig - f_cum
    m = f_cum + jnp.maximum(m0[..., None], lax.cummax(g, axis=2))
    a = jnp.exp(f_cum + m0[..., None] - m)
    causal = jnp.tril(jnp.ones((L, L), dtype=bool))
    log_d = (f_cum - m)[..., :, None] + g[..., None, :]
    d = jnp.exp(jnp.where(causal, log_d, -jnp.inf))
    qk = jnp.einsum('bhtd,bhsd->bhts', q, k) * d
    num = a[..., None] * jnp.einsum('bhtd,bhde->bhte', q, c0) + jnp.einsum('bhts,bhse->bhte', qk, v)
    den = a * jnp.einsum('bhtd,bhd->bht', q, n0) + jnp.sum(qk, axis=-1)
    h = num / jnp.maximum(jnp.abs(den), jnp.exp(-m))[..., None]
    d_last = d[..., -1, :]
    c1 = a[..., -1, None, None] * c0 + jnp.einsum('bhs,bhsd,bhse->bhde', d_last, k, v)
    n1 = a[..., -1, None] * n0 + jnp.einsum('bhs,bhsd->bhd', d_last, k)
    return (c1, n1, m[..., -1]), h


def conv_mlstm_core(p, gates, conv_buf, c0, n0, m0, b_ig, b_fg, conv_w, conv_b, cn_g, cn_b, mh_g):
    bsz, s_len, _ = p.shape
    cw, mw, nh = CONV_W, MLSTM_W, MLSTM_HEADS
    c_val = p[..., :cw]
    c_gate = p[..., cw:2 * cw]
    q = p[..., 2 * cw:2 * cw + mw]
    k = p[..., 2 * cw + mw:2 * cw + 2 * mw]
    v = p[..., 2 * cw + 2 * mw:2 * cw + 3 * mw]
    o = p[..., 2 * cw + 3 * mw:2 * cw + 4 * mw]
    ig = gates[..., :nh] + b_ig
    lf = jax.nn.log_sigmoid(gates[..., nh:2 * nh] + b_fg)
    u = c_val * jax.nn.sigmoid(c_gate)
    ext = jnp.concatenate([conv_buf, u], axis=1)
    y = lax.conv_general_dilated(ext, conv_w[:, None, :], window_strides=(1,),
                                 padding='VALID', dimension_numbers=('NWC', 'WIO', 'NWC'),
                                 feature_group_count=cw) + conv_b
    y = jax.nn.silu(group_layer_norm(y, cn_g, cn_b))
    new_buf = ext[:, ext.shape[1] - (CONV_K - 1):]
    L = MLSTM_CHUNK if s_len % MLSTM_CHUNK == 0 else s_len
    nc = s_len // L

    def heads(t):
        return t.reshape(bsz, nc, L, nh, MLSTM_DH).transpose(1, 0, 3, 2, 4)

    def gate_chunks(t):
        return t.reshape(bsz, nc, L, nh).transpose(1, 0, 3, 2)

    (c1, n1, m1), hs = lax.scan(mlstm_chunk, (c0, n0, m0),
                                (heads(q) * (MLSTM_DH ** -0.5), heads(k), heads(v), gate_chunks(ig), gate_chunks(lf)))
    hs = hs.transpose(1, 0, 3, 2, 4).reshape(bsz, s_len, nh, MLSTM_DH)
    hm = rms_norm(hs, mh_g.reshape(nh, MLSTM_DH)).reshape(bsz, s_len, mw) * jax.nn.sigmoid(o)
    return y, hm, new_buf, c1, n1, m1


def kernel(x_prompt, x_sample, state_conv, state_C, state_n, state_m, cache_k, cache_v, cache_logf, page_table,
           norm0_mix, w_in0, b_ig0, b_fg0, conv_w0, conv_b0, conv_norm_g0, conv_norm_b0, mlstm_norm_g0, w_out0,
           norm1_mix, w_in1, b_f1, q_norm_g1, k_norm_g1, w_out1,
           norm0_ffn, peer_wq0, peer_keys0, peer_u0, peer_v0,
           norm1_ffn, peer_wq1, peer_keys1, peer_u1, peer_v1):
    bp, sp, d = x_prompt.shape
    bs, ss, _ = x_sample.shape
    tp, ts = bp * sp, bs * ss
    x = jnp.concatenate([x_prompt.reshape(tp, d), x_sample.reshape(ts, d)], axis=0)

    n_main0 = 2 * CONV_W + 4 * MLSTM_W
    xn = rmsnorm_bf16(x, norm0_mix)
    p0 = mm_tokens(xn, w_in0[:, :n_main0].astype(BF16))
    g0 = gate_proj(x, norm0_mix, w_in0[:, n_main0:])
    mix = (b_ig0, b_fg0, conv_w0, conv_b0, conv_norm_g0, conv_norm_b0, mlstm_norm_g0)
    yp, hp, p_conv, p_C, p_n, p_m = conv_mlstm_core(
        p0[:tp].reshape(bp, sp, -1), g0[:tp].reshape(bp, sp, -1),
        jnp.zeros((bp, CONV_K - 1, CONV_W), F32), jnp.zeros((bp, MLSTM_HEADS, MLSTM_DH, MLSTM_DH), F32),
        jnp.zeros((bp, MLSTM_HEADS, MLSTM_DH), F32), jnp.zeros((bp, MLSTM_HEADS), F32), *mix)
    ys, hs, s_conv, s_C, s_n, s_m = conv_mlstm_core(
        p0[tp:].reshape(bs, ss, -1), g0[tp:].reshape(bs, ss, -1), state_conv, state_C, state_n, state_m, *mix)
    cat = jnp.concatenate([jnp.concatenate([yp, hp], axis=-1).reshape(tp, -1),
                           jnp.concatenate([ys, hs], axis=-1).reshape(ts, -1)], axis=0).astype(BF16)
    x = x + mm_tokens(cat, w_out0.astype(BF16))
    x = x + peer(x, norm0_ffn, peer_wq0, peer_keys0, peer_u0, peer_v0)

    xn = rmsnorm_bf16(x, norm1_mix)
    p1 = mm_tokens(xn, w_in1[:, :3 * FOX_W].astype(BF16))
    g1 = gate_proj(x, norm1_mix, w_in1[:, 3 * FOX_W:])
    lf = jax.nn.log_sigmoid(g1 + b_f1)
    q_bf, k = fox_norm(p1, q_norm_g1, k_norm_g1)
    v = p1[:, 2 * FOX_W:]
    f_cum = jnp.cumsum(lf[:tp].reshape(bp, sp, FOX_HEADS), axis=1).transpose(0, 2, 1).reshape(bp * FOX_HEADS, 1, sp)
    attn_p = fox_prompt(q_bf, k, p1, f_cum, bp, sp)
    attn_s = fox_sample(q_bf[tp:].reshape(bs, ss, FOX_W), k[tp:].reshape(bs, ss, FOX_W),
                        v[tp:].reshape(bs, ss, FOX_W), lf[tp:].reshape(bs, ss, FOX_HEADS),
                        cache_k, cache_v, cache_logf, page_table)
    attn = jnp.concatenate([attn_p, attn_s.reshape(ts, FOX_W).astype(BF16)], axis=0)
    x = x + mm_tokens(attn, w_out1.astype(BF16))
    x = x + peer(x, norm1_ffn, peer_wq1, peer_keys1, peer_u1, peer_v1)

    hd = (FOX_HEADS, FOX_DH)
    return (x[:tp].reshape(bp, sp, d), x[tp:].reshape(bs, ss, d), p_conv, p_C, p_n, p_m,
            k[:tp].reshape(bp, sp, *hd), v[:tp].reshape(bp, sp, *hd), lf[:tp].reshape(bp, sp, FOX_HEADS),
            s_conv, s_C, s_n, s_m,
            k[tp:].reshape(bs, ss, *hd), v[tp:].reshape(bs, ss, *hd), lf[tp:].reshape(bs, ss, FOX_HEADS))
```

```python
import functools
import math

import jax
import jax.numpy as jnp
import numpy as np
from jax import lax
from jax.experimental import pallas as pl
from jax.experimental.pallas import tpu as pltpu

F32 = jnp.float32
BF16 = jnp.bfloat16

D_MODEL = 2048
CONV_W = 1024
CONV_K = 31
CONV_GROUPS = 8
MLSTM_HEADS = 4
MLSTM_W = 1024
MLSTM_DH = 256
MLSTM_CHUNK = 128
FOX_HEADS = 16
FOX_DH = 128
FOX_W = 2048
PEER_HEADS = 8
PEER_NKEYS = 128
PEER_DK = 128
PEER_TOPK = 16
RMS_EPS = 1e-6
LN_EPS = 1e-5

LANES = 128
VMEM_LIMIT = 56 * 1024 * 1024
NEG_INF = float("-inf")


def _params(*sem):
    return pltpu.CompilerParams(dimension_semantics=sem, vmem_limit_bytes=VMEM_LIMIT)


def _rmsnorm_kernel(x_ref, g_ref, o_ref):
    x = x_ref[...]
    y = x * lax.rsqrt(jnp.mean(x * x, axis=-1, keepdims=True) + RMS_EPS)
    o_ref[...] = (y * g_ref[...]).astype(o_ref.dtype)


def rmsnorm_bf16(x, g):
    t, d = x.shape
    tm = _pick(t, (512, 256, 128))
    return pl.pallas_call(
        _rmsnorm_kernel,
        grid=(t // tm,),
        in_specs=[pl.BlockSpec((tm, d), lambda i: (i, 0)), pl.BlockSpec((1, d), lambda i: (0, 0))],
        out_specs=pl.BlockSpec((tm, d), lambda i: (i, 0)),
        out_shape=jax.ShapeDtypeStruct((t, d), BF16),
        compiler_params=_params("parallel"),
        name="rmsnorm",
    )(x, g.reshape(1, d))


def _gate_proj_kernel(x_ref, g_ref, w_ref, o_ref):
    x = x_ref[...]
    y = x * lax.rsqrt(jnp.mean(x * x, axis=-1, keepdims=True) + RMS_EPS) * g_ref[...]
    o_ref[...] = jnp.dot(y, w_ref[...], precision=lax.Precision.HIGHEST, preferred_element_type=F32)


def gate_proj(x, g, w):
    t, d = x.shape
    n = w.shape[1]
    tm = _pick(t, (256, 128))
    out = pl.pallas_call(
        _gate_proj_kernel,
        grid=(t // tm,),
        in_specs=[pl.BlockSpec((tm, d), lambda i: (i, 0)), pl.BlockSpec((1, d), lambda i: (0, 0)),
                  pl.BlockSpec((d, LANES), lambda i: (0, 0))],
        out_specs=pl.BlockSpec((tm, LANES), lambda i: (i, 0)),
        out_shape=jax.ShapeDtypeStruct((t, LANES), F32),
        compiler_params=_params("parallel"),
        name="gate_proj",
    )(x, g.reshape(1, d), jnp.pad(w, ((0, 0), (0, LANES - n))))
    return out[:, :n]


def _mm_kernel(x_ref, w_ref, o_ref):
    o_ref[...] = jnp.dot(x_ref[...], w_ref[...], preferred_element_type=F32).astype(o_ref.dtype)


def mm(x, w, tm, tn, out_dtype=F32):
    m, k = x.shape
    n = w.shape[1]
    assert m % tm == 0 and n % tn == 0, (m, n, tm, tn)
    return pl.pallas_call(
        _mm_kernel,
        grid=(n // tn, m // tm),
        in_specs=[pl.BlockSpec((tm, k), lambda j, i: (i, 0)), pl.BlockSpec((k, tn), lambda j, i: (0, j))],
        out_specs=pl.BlockSpec((tm, tn), lambda j, i: (i, j)),
        out_shape=jax.ShapeDtypeStruct((m, n), out_dtype),
        compiler_params=_params("parallel", "parallel"),
        name="mm",
    )(x, w)


def _pick(n, cands):
    for c in cands:
        if n % c == 0:
            return c
    raise ValueError(n)


def mm_tokens(x, w, out_dtype=F32):
    t = x.shape[0]
    n = w.shape[1]
    return mm(x, w, _pick(t, (512, 256, 128)), _pick(n, (1024, 512, 256, 128)), out_dtype)


def _topk_rows(s, k):
    n = s.shape[0]
    iota = lax.broadcasted_iota(jnp.int32, s.shape, 0)
    iota_k = lax.broadcasted_iota(jnp.int32, (k, s.shape[1]), 0)

    def body(r, carry):
        w, rank, vals = carry
        mx = jnp.max(w, axis=0, keepdims=True)
        idx = jnp.min(jnp.where(w == mx, iota, n), axis=0, keepdims=True)
        hit = iota == idx
        w = jnp.where(hit, NEG_INF, w)
        rank = jnp.where(hit, r, rank)
        vals = jnp.where(iota_k == r, mx, vals)
        return w, rank, vals

    init = (s, jnp.full(s.shape, k, jnp.int32), jnp.zeros((k, s.shape[1]), F32))
    _, rank, vals = lax.fori_loop(0, k, body, init)
    return vals, rank


def _peer_select_kernel(s_ref, rank2_ref, e2z_ref, nsel_ref, e1_ref):
    k = PEER_TOPK
    half = k // 2
    L = s_ref.shape[1]
    sub = lax.broadcasted_iota(jnp.int32, (half, L), 0)

    def head(h, _):
        base = pl.multiple_of(h * (2 * PEER_NKEYS), 2 * PEER_NKEYS)
        s1 = s_ref[pl.ds(base, PEER_NKEYS), :]
        s2 = s_ref[pl.ds(base + PEER_NKEYS, PEER_NKEYS), :]
        v1, rank1 = _topk_rows(s1, k)
        v2, rank2 = _topk_rows(s2, k)
        pieces, pos = [], []
        for r1 in range(half):
            pieces.append(v1[r1:r1 + 1, :] + v2[0:half, :])
            pos.append(sub + r1 * k)
        pieces.append(v1[0:1, :] + v2[half:k, :])
        pos.append(sub + half)
        pieces.append(v1[half:k, :] + v2[0:1, :])
        pos.append((sub + half) * k)
        cand = jnp.concatenate(pieces, axis=0)
        posid = jnp.concatenate(pos, axis=0)
        big = k * k

        def pick(_, carry):
            w, sel = carry
            mx = jnp.max(w, axis=0, keepdims=True)
            p = jnp.min(jnp.where(w == mx, posid, big), axis=0, keepdims=True)
            hit = posid == p
            return jnp.where(hit, NEG_INF, w), jnp.where(hit, 1.0, sel)

        _, sel = lax.fori_loop(0, k, pick, (cand, jnp.zeros(cand.shape, F32)))
        top = v1[0:1, :] + v2[0:1, :]
        z = jnp.sum(sel * jnp.exp(cand - top), axis=0, keepdims=True)
        counts = []
        for r1 in range(half):
            c = jnp.sum(sel[r1 * half:(r1 + 1) * half, :], axis=0, keepdims=True)
            if r1 == 0:
                c = c + jnp.sum(sel[half * half:half * half + half, :], axis=0, keepdims=True)
            counts.append(c)
        tail = sel[half * half + half:, :]
        for r1 in range(half, k):
            counts.append(tail[r1 - half:r1 - half + 1, :])
        nsel = jnp.zeros(s1.shape, F32)
        for r1 in range(k):
            nsel = jnp.where(rank1 == r1, counts[r1], nsel)
        rank2_ref[h] = rank2.astype(F32)
        nsel_ref[h] = nsel
        e1_ref[h] = jnp.exp(s1 - v1[0:1, :])
        e2z_ref[h] = jnp.exp(s2 - v2[0:1, :]) / z
        return 0

    lax.fori_loop(0, PEER_HEADS, head, 0)


def peer_select(s_t):
    rows, t = s_t.shape
    spec = pl.BlockSpec((PEER_HEADS, PEER_NKEYS, LANES), lambda i: (0, 0, i))
    shp = jax.ShapeDtypeStruct((PEER_HEADS, PEER_NKEYS, t), F32)
    return pl.pallas_call(
        _peer_select_kernel,
        grid=(t // LANES,),
        in_specs=[pl.BlockSpec((rows, LANES), lambda i: (0, i))],
        out_specs=[spec] * 4,
        out_shape=[shp] * 4,
        compiler_params=_params("parallel"),
        name="peer_select",
    )(s_t)


def _gelu(x):
    return 0.5 * x * (1.0 + lax.erf(x * (1.0 / math.sqrt(2.0))))


def _peer_dense_kernel(xn_ref, u_ref, vt_ref, rank2_ref, e2z_ref, nsel_ref, e1_ref, o_ref, a_ref):
    j = pl.program_id(1)
    nb = u_ref.shape[0] // PEER_NKEYS

    @pl.when(j == 0)
    def _():
        o_ref[...] = jnp.zeros_like(o_ref)

    for kb in range(nb):
        a = j * nb + kb
        rows = slice(kb * PEER_NKEYS, (kb + 1) * PEER_NKEYS)
        s = jnp.dot(u_ref[rows, :], xn_ref[...], preferred_element_type=F32)
        g = jnp.zeros(s.shape, F32)
        for h in range(PEER_HEADS):
            n_row = nsel_ref[h, pl.ds(a, 1), :]
            e_row = e1_ref[h, pl.ds(a, 1), :]
            g = g + jnp.where(rank2_ref[h] < n_row, e2z_ref[h], 0.0) * e_row
        a_ref[rows, :] = (_gelu(s) * g).astype(BF16)
    o_ref[...] += jnp.dot(vt_ref[...], a_ref[...], preferred_element_type=F32)


def peer_dense(xn_t, u_bf, vt_bf, rank2, e2z, nsel, e1, tm=512, te=512):
    d, t = xn_t.shape
    n_exp = u_bf.shape[0]
    sel_spec = pl.BlockSpec((PEER_HEADS, PEER_NKEYS, tm), lambda i, j: (0, 0, i))
    return pl.pallas_call(
        _peer_dense_kernel,
        grid=(t // tm, n_exp // te),
        in_specs=[pl.BlockSpec((d, tm), lambda i, j: (0, i)),
                  pl.BlockSpec((te, d), lambda i, j: (j, 0)),
                  pl.BlockSpec((d, te), lambda i, j: (0, j)),
                  sel_spec, sel_spec, sel_spec, sel_spec],
        out_specs=pl.BlockSpec((d, tm), lambda i, j: (0, i)),
        out_shape=jax.ShapeDtypeStruct((d, t), F32),
        scratch_shapes=[pltpu.VMEM((te, tm), BF16)],
        compiler_params=_params("parallel", "arbitrary"),
        name="peer_dense",
    )(xn_t, u_bf, vt_bf, rank2, e2z, nsel, e1)


def peer(x, g, w_q, sub_keys, u_tab, v_tab):
    t, d = x.shape
    xn = rmsnorm_bf16(x, g)
    xn_t = xn.T
    wq_t = w_q.T.astype(BF16)
    q_t = mm(wq_t, xn_t, _pick(wq_t.shape[0], (512, 256, 128)), _pick(t, (512, 256, 128)), BF16)
    hc = PEER_HEADS * 2
    dk2 = PEER_DK // 2
    keys = sub_keys.reshape(hc, PEER_NKEYS, dk2).astype(BF16)
    eye = jnp.eye(hc, dtype=BF16)
    kbd_t = (keys[:, :, None, :] * eye[:, None, :, None]).reshape(hc * PEER_NKEYS, hc * dk2)
    s_t = mm(kbd_t, q_t, _pick(kbd_t.shape[0], (512, 256, 128)), _pick(t, (512, 256, 128)))
    rank2, e2z, nsel, e1 = peer_select(s_t)
    out_t = peer_dense(xn_t, u_tab.astype(BF16), v_tab.T.astype(BF16), rank2, e2z, nsel, e1,
                       tm=_pick(t, (512, 256, 128)))
    return out_t.T


def _fox_norm_kernel(p_ref, gq_ref, gk_ref, q_ref, k_ref):
    scale = FOX_DH ** -0.5
    for h in range(FOX_HEADS):
        lo, hi = h * FOX_DH, (h + 1) * FOX_DH
        x = p_ref[:, lo:hi]
        y = x * lax.rsqrt(jnp.mean(x * x, axis=-1, keepdims=True) + RMS_EPS) * gq_ref[...]
        q_ref[:, lo:hi] = (y * scale).astype(q_ref.dtype)
        x = p_ref[:, FOX_W + lo:FOX_W + hi]
        k_ref[:, lo:hi] = x * lax.rsqrt(jnp.mean(x * x, axis=-1, keepdims=True) + RMS_EPS) * gk_ref[...]


def fox_norm(p, g_q, g_k):
    t = p.shape[0]
    tm = _pick(t, (256, 128))
    return pl.pallas_call(
        _fox_norm_kernel,
        grid=(t // tm,),
        in_specs=[pl.BlockSpec((tm, 2 * FOX_W), lambda i: (i, 0)),
                  pl.BlockSpec((1, FOX_DH), lambda i: (0, 0)), pl.BlockSpec((1, FOX_DH), lambda i: (0, 0))],
        out_specs=[pl.BlockSpec((tm, FOX_W), lambda i: (i, 0))] * 2,
        out_shape=[jax.ShapeDtypeStruct((t, FOX_W), BF16), jax.ShapeDtypeStruct((t, FOX_W), F32)],
        compiler_params=_params("parallel"),
        name="fox_norm",
    )(p, g_q.reshape(1, FOX_DH), g_k.reshape(1, FOX_DH))


def _fox_prompt_kernel(qi_ref, ki_ref, q_ref, k_ref, v_ref, f_ref, o_ref, m_ref, l_ref, acc_ref, *, tq, tk):
    n = pl.program_id(2)
    i = qi_ref[n]
    kk = ki_ref[n]

    @pl.when(kk == 0)
    def _():
        m_ref[...] = jnp.full_like(m_ref, NEG_INF)
        l_ref[...] = jnp.zeros_like(l_ref)
        acc_ref[...] = jnp.zeros_like(acc_ref)

    def step(masked):
        s = lax.dot_general(q_ref[...], k_ref[...].astype(BF16), (((1,), (1,)), ((), ())),
                            preferred_element_type=F32)
        s = s - f_ref[0]
        if masked:
            q_pos = i * tq + lax.broadcasted_iota(jnp.int32, s.shape, 0)
            k_pos = kk * tk + lax.broadcasted_iota(jnp.int32, s.shape, 1)
            s = jnp.where(q_pos >= k_pos, s, NEG_INF)
        m_prev = m_ref[...]
        m_new = jnp.maximum(m_prev, jnp.max(s, axis=1, keepdims=True))
        alpha = jnp.exp(m_prev - m_new)
        p = jnp.exp(s - m_new)
        l_ref[...] = alpha * l_ref[...] + jnp.sum(p, axis=1, keepdims=True)
        acc_ref[...] = alpha * acc_ref[...] + jnp.dot(p.astype(BF16), v_ref[...].astype(BF16),
                                                      preferred_element_type=F32)
        m_ref[...] = m_new

    on_diag = (kk + 1) * tk > i * tq + 1

    @pl.when(on_diag)
    def _():
        step(True)

    @pl.when(jnp.logical_not(on_diag))
    def _():
        step(False)

    @pl.when(kk == (i * tq + tq - 1) // tk)
    def _():
        o_ref[...] = (acc_ref[...] / l_ref[...]).astype(o_ref.dtype)


def fox_prompt(q_bf, k, p, f_cum, bsz, s_len, tq=1024, tk=512):
    nq, nk = s_len // tq, s_len // tk
    pairs = [(i, kk) for i in range(nq) for kk in range((i * tq + tq - 1) // tk + 1)]
    qi = jnp.asarray([a for a, _ in pairs], jnp.int32)
    ki = jnp.asarray([b for _, b in pairs], jnp.int32)
    v_col0 = 2 * FOX_W // FOX_DH
    grid_spec = pltpu.PrefetchScalarGridSpec(
        num_scalar_prefetch=2,
        grid=(bsz, FOX_HEADS, len(pairs)),
        in_specs=[pl.BlockSpec((tq, FOX_DH), lambda b, h, n, qi, ki: (b * nq + qi[n], h)),
                  pl.BlockSpec((tk, FOX_DH), lambda b, h, n, qi, ki: (b * nk + ki[n], h)),
                  pl.BlockSpec((tk, FOX_DH), lambda b, h, n, qi, ki: (b * nk + ki[n], v_col0 + h)),
                  pl.BlockSpec((1, 1, tk), lambda b, h, n, qi, ki: (b * FOX_HEADS + h, 0, ki[n]))],
        out_specs=pl.BlockSpec((tq, FOX_DH), lambda b, h, n, qi, ki: (b * nq + qi[n], h)),
        scratch_shapes=[pltpu.VMEM((tq, 1), F32), pltpu.VMEM((tq, 1), F32), pltpu.VMEM((tq, FOX_DH), F32)],
    )
    return pl.pallas_call(
        functools.partial(_fox_prompt_kernel, tq=tq, tk=tk),
        grid_spec=grid_spec,
        out_shape=jax.ShapeDtypeStruct((bsz * s_len, FOX_W), BF16),
        compiler_params=_params("parallel", "parallel", "arbitrary"),
        name="fox_prompt",
    )(qi, ki, q_bf, k, p, f_cum)


SUB = 8


def _logf_cumsum_kernel(pt_ref, lf_ref, o_ref, carry_ref):
    @pl.when(pl.program_id(1) == 0)
    def _():
        carry_ref[...] = jnp.zeros_like(carry_ref)

    page = lf_ref.shape[1]
    tri = (lax.broadcasted_iota(jnp.int32, (page, page), 1)
           <= lax.broadcasted_iota(jnp.int32, (page, page), 0)).astype(F32)
    f = jnp.dot(tri, lf_ref[0], precision=lax.Precision.HIGHEST, preferred_element_type=F32) + carry_ref[...]
    carry_ref[...] = f[page - 1:page, :]
    o_ref[0, 0] = f


def paged_logf_cumsum(cache_logf, page_table):
    bsz, n_pages = page_table.shape
    _, page, nh = cache_logf.shape
    grid_spec = pltpu.PrefetchScalarGridSpec(
        num_scalar_prefetch=1,
        grid=(bsz, n_pages),
        in_specs=[pl.BlockSpec((1, page, nh), lambda b, p, pt: (pt[b, p], 0, 0))],
        out_specs=pl.BlockSpec((1, 1, page, nh), lambda b, p, pt: (b, p, 0, 0)),
        scratch_shapes=[pltpu.VMEM((1, nh), F32)],
    )
    return pl.pallas_call(
        _logf_cumsum_kernel,
        grid_spec=grid_spec,
        out_shape=jax.ShapeDtypeStruct((bsz, n_pages, page, nh), F32),
        compiler_params=_params("parallel", "arbitrary"),
        name="logf_cumsum",
    )(page_table, cache_logf)


def _fox_sample_kernel(pt_ref, q_ref, *refs, n_groups, group):
    kc_refs, vc_refs = refs[:group], refs[group:2 * group]
    f_ref, mask_ref, kn_ref, vn_ref, fn_ref, maskn_ref, o_ref, m_ref, l_ref, acc_ref = refs[2 * group:]
    pg = pl.program_id(1)

    @pl.when(pg == 0)
    def _():
        m_ref[...] = jnp.full_like(m_ref, NEG_INF)
        l_ref[...] = jnp.zeros_like(l_ref)
        acc_ref[...] = jnp.zeros_like(acc_ref)

    def attend(kvb):
        ss = [lax.dot_general(q_ref[0], k2, (((1,), (1,)), ((), ())), preferred_element_type=F32) + b
              for k2, _, b in kvb]
        m_prev = m_ref[...]
        m_new = m_prev
        for s in ss:
            m_new = jnp.maximum(m_new, jnp.max(s, axis=1, keepdims=True))
        alpha = jnp.exp(m_prev - m_new)
        l_new = alpha * l_ref[...]
        acc = alpha * acc_ref[...]
        for s, (_, v2, _) in zip(ss, kvb):
            p = jnp.exp(s - m_new)
            l_new = l_new + jnp.sum(p, axis=1, keepdims=True)
            acc = acc + jnp.dot(p.astype(BF16), v2, preferred_element_type=F32)
        l_ref[...] = l_new
        acc_ref[...] = acc
        m_ref[...] = m_new

    @pl.when(pg < n_groups)
    def _():
        rows = kc_refs[0].shape[0] * kc_refs[0].shape[1]
        attend([(kc[...].reshape(rows, FOX_DH).astype(BF16), vc[...].reshape(rows, FOX_DH).astype(BF16),
                 mask_ref[...] - f_ref[0, g])
                for g, (kc, vc) in enumerate(zip(kc_refs, vc_refs))])

    @pl.when(pg == n_groups)
    def _():
        attend([(kn_ref[0].astype(BF16), vn_ref[0].astype(BF16), maskn_ref[...] - fn_ref[0])])
        o_ref[0] = acc_ref[...] / l_ref[...]


def fox_sample(q_bf, k_new, v_new, lf_new, cache_k, cache_v, cache_logf, page_table):
    bsz, t_len, _ = q_bf.shape
    n_pool, page = cache_k.shape[:2]
    n_pages = page_table.shape[1]
    nh, dh = FOX_HEADS, FOX_DH
    rows = nh * t_len
    lanes = page * nh
    lanes_new = SUB * nh
    qf = q_bf.reshape(bsz, t_len, nh, dh).transpose(0, 2, 1, 3).reshape(bsz, rows, dh)
    f_pages = paged_logf_cumsum(cache_logf, page_table)
    f_rows = f_pages.reshape(bsz, n_pages, 1, lanes)
    pad = ((0, 0), (0, SUB - t_len), (0, 0))
    kn = jnp.pad(k_new, pad).reshape(bsz, lanes_new, dh)
    vn = jnp.pad(v_new, pad).reshape(bsz, lanes_new, dh)
    fn = (f_pages[:, -1, -1, :][:, None, :] + jnp.cumsum(jnp.pad(lf_new, pad), axis=1)).reshape(bsz, 1, lanes_new)
    row_head, row_q = np.arange(rows) // t_len, np.arange(rows) % t_len
    mask = np.where(row_head[:, None] == (np.arange(lanes) % nh)[None, :], 0.0, NEG_INF).astype(np.float32)
    key_n, head_n = np.arange(lanes_new) // nh, np.arange(lanes_new) % nh
    ok = (row_head[:, None] == head_n[None, :]) & (key_n[None, :] <= row_q[:, None]) & (key_n[None, :] < t_len)
    mask_new = np.where(ok, 0.0, NEG_INF).astype(np.float32)
    group = _pick(n_pages, (4, 2, 1))
    n_groups = n_pages // group
    last = n_groups - 1

    def cache_spec(g):
        return pl.BlockSpec((page, nh, dh), lambda b, p, pt: (pt[b, jnp.minimum(p, last) * group + g], 0, 0))

    cache_specs = [cache_spec(g) for g in range(group)]
    kc3 = cache_k.reshape(n_pool * page, nh, dh)
    vc3 = cache_v.reshape(n_pool * page, nh, dh)
    grid_spec = pltpu.PrefetchScalarGridSpec(
        num_scalar_prefetch=1,
        grid=(bsz, n_groups + 1),
        in_specs=[pl.BlockSpec((1, rows, dh), lambda b, p, pt: (b, 0, 0)),
                  *cache_specs, *cache_specs,
                  pl.BlockSpec((1, group, 1, lanes), lambda b, p, pt: (b, jnp.minimum(p, last), 0, 0)),
                  pl.BlockSpec((rows, lanes), lambda b, p, pt: (0, 0)),
                  pl.BlockSpec((1, lanes_new, dh), lambda b, p, pt: (b, 0, 0)),
                  pl.BlockSpec((1, lanes_new, dh), lambda b, p, pt: (b, 0, 0)),
                  pl.BlockSpec((1, 1, lanes_new), lambda b, p, pt: (b, 0, 0)),
                  pl.BlockSpec((rows, lanes_new), lambda b, p, pt: (0, 0))],
        out_specs=pl.BlockSpec((1, rows, dh), lambda b, p, pt: (b, 0, 0)),
        scratch_shapes=[pltpu.VMEM((rows, 1), F32), pltpu.VMEM((rows, 1), F32), pltpu.VMEM((rows, dh), F32)],
    )
    out = pl.pallas_call(
        functools.partial(_fox_sample_kernel, n_groups=n_groups, group=group),
        grid_spec=grid_spec,
        out_shape=jax.ShapeDtypeStruct((bsz, rows, dh), F32),
        compiler_params=_params("parallel", "arbitrary"),
        name="fox_sample",
    )(page_table, qf, *([kc3] * group), *([vc3] * group),
      f_rows, jnp.asarray(mask), kn, vn, fn, jnp.asarray(mask_new))
    return out.reshape(bsz, nh, t_len, dh).transpose(0, 2, 1, 3).reshape(bsz, t_len, nh * dh)


def rms_norm(x, g):
    xf = x.astype(F32)
    y = xf * lax.rsqrt(jnp.mean(jnp.square(xf), axis=-1, keepdims=True) + RMS_EPS)
    return (y * g.astype(F32)).astype(x.dtype)


def group_layer_norm(x, g, b):
    xf = x.astype(F32).reshape(x.shape[:-1] + (CONV_GROUPS, -1))
    mu = jnp.mean(xf, axis=-1, keepdims=True)
    var = jnp.mean(jnp.square(xf - mu), axis=-1, keepdims=True)
    y = ((xf - mu) * lax.rsqrt(var + LN_EPS)).reshape(x.shape)
    return (y * g.astype(F32) + b.astype(F32)).astype(x.dtype)


def mlstm_chunk(carry, inp):
    c0, n0, m0 = carry
    q, k, v, ig, lf = inp
    L = q.shape[2]
    f_cum = jnp.cumsum(lf, axis=-1)
    g = ig - f_cum
    m = f_cum + jnp.maximum(m0[..., None], lax.cummax(g, axis=2))
    a = jnp.exp(f_cum + m0[..., None] - m)
    causal = jnp.tril(jnp.ones((L, L), dtype=bool))
    log_d = (f_cum - m)[..., :, None] + g[..., None, :]
    d = jnp.exp(jnp.where(causal, log_d, -jnp.inf))
    qk = jnp.einsum('bhtd,bhsd->bhts', q, k) * d
    num = a[..., None] * jnp.einsum('bhtd,bhde->bhte', q, c0) + jnp.einsum('bhts,bhse->bhte', qk, v)
    den = a * jnp.einsum('bhtd,bhd->bht', q, n0) + jnp.sum(qk, axis=-1)
    h = num / jnp.maximum(jnp.abs(den), jnp.exp(-m))[..., None]
    d_last = d[..., -1, :]
    c1 = a[..., -1, None, None] * c0 + jnp.einsum('bhs,bhsd,bhse->bhde', d_last, k, v)
    n1 = a[..., -1, None] * n0 + jnp.einsum('bhs,bhsd->bhd', d_last, k)
    return (c1, n1, m[..., -1]), h


def conv_mlstm_core(p, gates, conv_buf, c0, n0, m0, b_ig, b_fg, conv_w, conv_b, cn_g, cn_b, mh_g):
    bsz, s_len, _ = p.shape
    cw, mw, nh = CONV_W, MLSTM_W, MLSTM_HEADS
    c_val = p[..., :cw]
    c_gate = p[..., cw:2 * cw]
    q = p[..., 2 * cw:2 * cw + mw]
    k = p[..., 2 * cw + mw:2 * cw + 2 * mw]
    v = p[..., 2 * cw + 2 * mw:2 * cw + 3 * mw]
    o = p[..., 2 * cw + 3 * mw:2 * cw + 4 * mw]
    ig = gates[..., :nh] + b_ig
    lf = jax.nn.log_sigmoid(gates[..., nh:2 * nh] + b_fg)
    u = c_val * jax.nn.sigmoid(c_gate)
    ext = jnp.concatenate([conv_buf, u], axis=1)
    y = lax.conv_general_dilated(ext, conv_w[:, None, :], window_strides=(1,),
                                 padding='VALID', dimension_numbers=('NWC', 'WIO', 'NWC'),
                                 feature_group_count=cw) + conv_b
    y = jax.nn.silu(group_layer_norm(y, cn_g, cn_b))
    new_buf = ext[:, ext.shape[1] - (CONV_K - 1):]
    L = MLSTM_CHUNK if s_len % MLSTM_CHUNK == 0 else s_len
    nc = s_len // L

    def heads(t):
        return t.reshape(bsz, nc, L, nh, MLSTM_DH).transpose(1, 0, 3, 2, 4)

    def gate_chunks(t):
        return t.reshape(bsz, nc, L, nh).transpose(1, 0, 3, 2)

    (c1, n1, m1), hs = lax.scan(mlstm_chunk, (c0, n0, m0),
                                (heads(q) * (MLSTM_DH ** -0.5), heads(k), heads(v), gate_chunks(ig), gate_chunks(lf)))
    hs = hs.transpose(1, 0, 3, 2, 4).reshape(bsz, s_len, nh, MLSTM_DH)
    hm = rms_norm(hs, mh_g.reshape(nh, MLSTM_DH)).reshape(bsz, s_len, mw) * jax.nn.sigmoid(o)
    return y, hm, new_buf, c1, n1, m1


def kernel(x_prompt, x_sample, state_conv, state_C, state_n, state_m, cache_k, cache_v, cache_logf, page_table,
           norm0_mix, w_in0, b_ig0, b_fg0, conv_w0, conv_b0, conv_norm_g0, conv_norm_b0, mlstm_norm_g0, w_out0,
           norm1_mix, w_in1, b_f1, q_norm_g1, k_norm_g1, w_out1,
           norm0_ffn, peer_wq0, peer_keys0, peer_u0, peer_v0,
           norm1_ffn, peer_wq1, peer_keys1, peer_u1, peer_v1):
    bp, sp, d = x_prompt.shape
    bs, ss, _ = x_sample.shape
    tp, ts = bp * sp, bs * ss
    x = jnp.concatenate([x_prompt.reshape(tp, d), x_sample.reshape(ts, d)], axis=0)

    n_main0 = 2 * CONV_W + 4 * MLSTM_W
    xn = rmsnorm_bf16(x, norm0_mix)
    p0 = mm_tokens(xn, w_in0[:, :n_main0].astype(BF16))
    g0 = gate_proj(x, norm0_mix, w_in0[:, n_main0:])
    mix = (b_ig0, b_fg0, conv_w0, conv_b0, conv_norm_g0, conv_norm_b0, mlstm_norm_g0)
    yp, hp, p_conv, p_C, p_n, p_m = conv_mlstm_core(
        p0[:tp].reshape(bp, sp, -1), g0[:tp].reshape(bp, sp, -1),
        jnp.zeros((bp, CONV_K - 1, CONV_W), F32), jnp.zeros((bp, MLSTM_HEADS, MLSTM_DH, MLSTM_DH), F32),
        jnp.zeros((bp, MLSTM_HEADS, MLSTM_DH), F32), jnp.zeros((bp, MLSTM_HEADS), F32), *mix)
    ys, hs, s_conv, s_C, s_n, s_m = conv_mlstm_core(
        p0[tp:].reshape(bs, ss, -1), g0[tp:].reshape(bs, ss, -1), state_conv, state_C, state_n, state_m, *mix)
    cat = jnp.concatenate([jnp.concatenate([yp, hp], axis=-1).reshape(tp, -1),
                           jnp.concatenate([ys, hs], axis=-1).reshape(ts, -1)], axis=0).astype(BF16)
    x = x + mm_tokens(cat, w_out0.astype(BF16))
    x = x + peer(x, norm0_ffn, peer_wq0, peer_keys0, peer_u0, peer_v0)

    xn = rmsnorm_bf16(x, norm1_mix)
    p1 = mm_tokens(xn, w_in1[:, :3 * FOX_W].astype(BF16))
    g1 = gate_proj(x, norm1_mix, w_in1[:, 3 * FOX_W:])
    lf = jax.nn.log_sigmoid(g1 + b_f1)
    q_bf, k = fox_norm(p1, q_norm_g1, k_norm_g1)
    v = p1[:, 2 * FOX_W:]
    f_cum = jnp.cumsum(lf[:tp].reshape(bp, sp, FOX_HEADS), axis=1).transpose(0, 2, 1).reshape(bp * FOX_HEADS, 1, sp)
    attn_p = fox_prompt(q_bf, k, p1, f_cum, bp, sp)
    attn_s = fox_sample(q_bf[tp:].reshape(bs, ss, FOX_W), k[tp:].reshape(bs, ss, FOX_W),
                        v[tp:].reshape(bs, ss, FOX_W), lf[tp:].reshape(bs, ss, FOX_HEADS),
                        cache_k, cache_v, cache_logf, page_table)
    attn = jnp.concatenate([attn_p, attn_s.reshape(ts, FOX_W).astype(BF16)], axis=0)
    x = x + mm_tokens(attn, w_out1.astype(BF16))
    x = x + peer(x, norm1_ffn, peer_wq1, peer_keys1, peer_u1, peer_v1)

    hd = (FOX_HEADS, FOX_DH)
    return (x[:tp].reshape(bp, sp, d), x[tp:].reshape(bs, ss, d), p_conv, p_C, p_n, p_m,
            k[:tp].reshape(bp, sp, *hd), v[:tp].reshape(bp, sp, *hd), lf[:tp].reshape(bp, sp, FOX_HEADS),
            s_conv, s_C, s_n, s_m,
            k[tp:].reshape(bs, ss, *hd), v[tp:].reshape(bs, ss, *hd), lf[tp:].reshape(bs, ss, FOX_HEADS))
```

```python
import functools
import math

import jax
import jax.numpy as jnp
import numpy as np
from jax import lax
from jax.experimental import pallas as pl
from jax.experimental.pallas import tpu as pltpu

F32 = jnp.float32
BF16 = jnp.bfloat16

D_MODEL = 2048
CONV_W = 1024
CONV_K = 31
CONV_GROUPS = 8
MLSTM_HEADS = 4
MLSTM_W = 1024
MLSTM_DH = 256
MLSTM_CHUNK = 128
FOX_HEADS = 16
FOX_DH = 128
FOX_W = 2048
PEER_HEADS = 8
PEER_NKEYS = 128
PEER_DK = 128
PEER_TOPK = 16
RMS_EPS = 1e-6
LN_EPS = 1e-5

LANES = 128
VMEM_LIMIT = 56 * 1024 * 1024
NEG_INF = float("-inf")


def _params(*sem):
    return pltpu.CompilerParams(dimension_semantics=sem, vmem_limit_bytes=VMEM_LIMIT)


def _rmsnorm_kernel(x_ref, g_ref, o_ref):
    x = x_ref[...]
    y = x * lax.rsqrt(jnp.mean(x * x, axis=-1, keepdims=True) + RMS_EPS)
    o_ref[...] = (y * g_ref[...]).astype(o_ref.dtype)


def rmsnorm_bf16(x, g):
    t, d = x.shape
    tm = _pick(t, (512, 256, 128))
    return pl.pallas_call(
        _rmsnorm_kernel,
        grid=(t // tm,),
        in_specs=[pl.BlockSpec((tm, d), lambda i: (i, 0)), pl.BlockSpec((1, d), lambda i: (0, 0))],
        out_specs=pl.BlockSpec((tm, d), lambda i: (i, 0)),
        out_shape=jax.ShapeDtypeStruct((t, d), BF16),
        compiler_params=_params("parallel"),
        name="rmsnorm",
    )(x, g.reshape(1, d))


def _gate_proj_kernel(x_ref, g_ref, w_ref, o_ref):
    x = x_ref[...]
    y = x * lax.rsqrt(jnp.mean(x * x, axis=-1, keepdims=True) + RMS_EPS) * g_ref[...]
    o_ref[...] = jnp.dot(y, w_ref[...], precision=lax.Precision.HIGHEST, preferred_element_type=F32)


def gate_proj(x, g, w):
    t, d = x.shape
    n = w.shape[1]
    tm = _pick(t, (256, 128))
    out = pl.pallas_call(
        _gate_proj_kernel,
        grid=(t // tm,),
        in_specs=[pl.BlockSpec((tm, d), lambda i: (i, 0)), pl.BlockSpec((1, d), lambda i: (0, 0)),
                  pl.BlockSpec((d, LANES), lambda i: (0, 0))],
        out_specs=pl.BlockSpec((tm, LANES), lambda i: (i, 0)),
        out_shape=jax.ShapeDtypeStruct((t, LANES), F32),
        compiler_params=_params("parallel"),
        name="gate_proj",
    )(x, g.reshape(1, d), jnp.pad(w, ((0, 0), (0, LANES - n))))
    return out[:, :n]


def _mm_kernel(x_ref, w_ref, o_ref):
    o_ref[...] = jnp.dot(x_ref[...], w_ref[...], preferred_element_type=F32).astype(o_ref.dtype)


def mm(x, w, tm, tn, out_dtype=F32):
    m, k = x.shape
    n = w.shape[1]
    assert m % tm == 0 and n % tn == 0, (m, n, tm, tn)
    return pl.pallas_call(
        _mm_kernel,
        grid=(n // tn, m // tm),
        in_specs=[pl.BlockSpec((tm, k), lambda j, i: (i, 0)), pl.BlockSpec((k, tn), lambda j, i: (0, j))],
        out_specs=pl.BlockSpec((tm, tn), lambda j, i: (i, j)),
        out_shape=jax.ShapeDtypeStruct((m, n), out_dtype),
        compiler_params=_params("parallel", "parallel"),
        name="mm",
    )(x, w)


def _mm2_kernel(x1_ref, x2_ref, w1_ref, w2_ref, o_ref):
    o_ref[...] = (jnp.dot(x1_ref[...], w1_ref[...], preferred_element_type=F32)
                  + jnp.dot(x2_ref[...], w2_ref[...], preferred_element_type=F32))


def mm2(x1, x2, w1, w2, tm, tn):
    m, k1 = x1.shape
    k2 = x2.shape[1]
    n = w1.shape[1]
    return pl.pallas_call(
        _mm2_kernel,
        grid=(n // tn, m // tm),
        in_specs=[pl.BlockSpec((tm, k1), lambda j, i: (i, 0)), pl.BlockSpec((tm, k2), lambda j, i: (i, 0)),
                  pl.BlockSpec((k1, tn), lambda j, i: (0, j)), pl.BlockSpec((k2, tn), lambda j, i: (0, j))],
        out_specs=pl.BlockSpec((tm, tn), lambda j, i: (i, j)),
        out_shape=jax.ShapeDtypeStruct((m, n), F32),
        compiler_params=_params("parallel", "parallel"),
        name="mm2",
    )(x1, x2, w1, w2)


def _pick(n, cands):
    for c in cands:
        if n % c == 0:
            return c
    raise ValueError(n)


def mm_tokens(x, w, out_dtype=F32):
    t = x.shape[0]
    n = w.shape[1]
    return mm(x, w, _pick(t, (512, 256, 128)), _pick(n, (1024, 512, 256, 128)), out_dtype)


def _topk_rows(s, k):
    n = s.shape[0]
    iota = lax.broadcasted_iota(jnp.int32, s.shape, 0)
    iota_k = lax.broadcasted_iota(jnp.int32, (k, s.shape[1]), 0)

    def body(r, carry):
        w, rank, vals = carry
        mx = jnp.max(w, axis=0, keepdims=True)
        idx = jnp.min(jnp.where(w == mx, iota, n), axis=0, keepdims=True)
        hit = iota == idx
        w = jnp.where(hit, NEG_INF, w)
        rank = jnp.where(hit, r, rank)
        vals = jnp.where(iota_k == r, mx, vals)
        return w, rank, vals

    init = (s, jnp.full(s.shape, k, jnp.int32), jnp.zeros((k, s.shape[1]), F32))
    _, rank, vals = lax.fori_loop(0, k, body, init)
    return vals, rank


def _peer_select_kernel(s_ref, rank2_ref, e2z_ref, nsel_ref, e1_ref):
    k = PEER_TOPK
    half = k // 2
    L = s_ref.shape[1]
    sub = lax.broadcasted_iota(jnp.int32, (half, L), 0)

    def head(h, _):
        base = pl.multiple_of(h * (2 * PEER_NKEYS), 2 * PEER_NKEYS)
        s1 = s_ref[pl.ds(base, PEER_NKEYS), :]
        s2 = s_ref[pl.ds(base + PEER_NKEYS, PEER_NKEYS), :]
        v1, rank1 = _topk_rows(s1, k)
        v2, rank2 = _topk_rows(s2, k)
        pieces, pos = [], []
        for r1 in range(half):
            pieces.append(v1[r1:r1 + 1, :] + v2[0:half, :])
            pos.append(sub + r1 * k)
        pieces.append(v1[0:1, :] + v2[half:k, :])
        pos.append(sub + half)
        pieces.append(v1[half:k, :] + v2[0:1, :])
        pos.append((sub + half) * k)
        cand = jnp.concatenate(pieces, axis=0)
        posid = jnp.concatenate(pos, axis=0)
        big = k * k

        def pick(_, carry):
            w, sel = carry
            mx = jnp.max(w, axis=0, keepdims=True)
            p = jnp.min(jnp.where(w == mx, posid, big), axis=0, keepdims=True)
            hit = posid == p
            return jnp.where(hit, NEG_INF, w), jnp.where(hit, 1.0, sel)

        _, sel = lax.fori_loop(0, k, pick, (cand, jnp.zeros(cand.shape, F32)))
        top = v1[0:1, :] + v2[0:1, :]
        z = jnp.sum(sel * jnp.exp(cand - top), axis=0, keepdims=True)
        counts = []
        for r1 in range(half):
            c = jnp.sum(sel[r1 * half:(r1 + 1) * half, :], axis=0, keepdims=True)
            if r1 == 0:
                c = c + jnp.sum(sel[half * half:half * half + half, :], axis=0, keepdims=True)
            counts.append(c)
        tail = sel[half * half + half:, :]
        for r1 in range(half, k):
            counts.append(tail[r1 - half:r1 - half + 1, :])
        nsel = jnp.zeros(s1.shape, F32)
        for r1 in range(k):
            nsel = jnp.where(rank1 == r1, counts[r1], nsel)
        rank2_ref[h] = rank2.astype(F32)
        nsel_ref[h] = nsel
        e1_ref[h] = jnp.exp(s1 - v1[0:1, :])
        e2z_ref[h] = jnp.exp(s2 - v2[0:1, :]) / z
        return 0

    lax.fori_loop(0, PEER_HEADS, head, 0)


def peer_select(s_t):
    rows, t = s_t.shape
    spec = pl.BlockSpec((PEER_HEADS, PEER_NKEYS, LANES), lambda i: (0, 0, i))
    shp = jax.ShapeDtypeStruct((PEER_HEADS, PEER_NKEYS, t), F32)
    return pl.pallas_call(
        _peer_select_kernel,
        grid=(t // LANES,),
        in_specs=[pl.BlockSpec((rows, LANES), lambda i: (0, i))],
        out_specs=[spec] * 4,
        out_shape=[shp] * 4,
        compiler_params=_params("parallel"),
        name="peer_select",
    )(s_t)


def _gelu(x):
    return 0.5 * x * (1.0 + lax.erf(x * (1.0 / math.sqrt(2.0))))


def _peer_dense_kernel(xn_ref, u_ref, vt_ref, rank2_ref, e2z_ref, nsel_ref, e1_ref, o_ref, a_ref):
    j = pl.program_id(1)
    nb = u_ref.shape[0] // PEER_NKEYS

    @pl.when(j == 0)
    def _():
        o_ref[...] = jnp.zeros_like(o_ref)

    for kb in range(nb):
        a = j * nb + kb
        rows = slice(kb * PEER_NKEYS, (kb + 1) * PEER_NKEYS)
        s = jnp.dot(u_ref[rows, :], xn_ref[...], preferred_element_type=F32)
        g = jnp.zeros(s.shape, F32)
        for h in range(PEER_HEADS):
            n_row = nsel_ref[h, pl.ds(a, 1), :]
            e_row = e1_ref[h, pl.ds(a, 1), :]
            g = g + jnp.where(rank2_ref[h] < n_row, e2z_ref[h], 0.0) * e_row
        a_ref[rows, :] = (_gelu(s) * g).astype(BF16)
    o_ref[...] += jnp.dot(vt_ref[...], a_ref[...], preferred_element_type=F32)


def peer_dense(xn_t, u_bf, vt_bf, rank2, e2z, nsel, e1, tm=512, te=512):
    d, t = xn_t.shape
    n_exp = u_bf.shape[0]
    sel_spec = pl.BlockSpec((PEER_HEADS, PEER_NKEYS, tm), lambda i, j: (0, 0, i))
    return pl.pallas_call(
        _peer_dense_kernel,
        grid=(t // tm, n_exp // te),
        in_specs=[pl.BlockSpec((d, tm), lambda i, j: (0, i)),
                  pl.BlockSpec((te, d), lambda i, j: (j, 0)),
                  pl.BlockSpec((d, te), lambda i, j: (0, j)),
                  sel_spec, sel_spec, sel_spec, sel_spec],
        out_specs=pl.BlockSpec((d, tm), lambda i, j: (0, i)),
        out_shape=jax.ShapeDtypeStruct((d, t), F32),
        scratch_shapes=[pltpu.VMEM((te, tm), BF16)],
        compiler_params=_params("parallel", "arbitrary"),
        name="peer_dense",
    )(xn_t, u_bf, vt_bf, rank2, e2z, nsel, e1)


def peer(x, g, w_q, sub_keys, u_tab, v_tab):
    t, d = x.shape
    xn = rmsnorm_bf16(x, g)
    xn_t = xn.T
    wq_t = w_q.T.astype(BF16)
    q_t = mm(wq_t, xn_t, _pick(wq_t.shape[0], (512, 256, 128)), _pick(t, (512, 256, 128)), BF16)
    hc = PEER_HEADS * 2
    dk2 = PEER_DK // 2
    keys = sub_keys.reshape(hc, PEER_NKEYS, dk2).astype(BF16)
    eye = jnp.eye(hc, dtype=BF16)
    kbd_t = (keys[:, :, None, :] * eye[:, None, :, None]).reshape(hc * PEER_NKEYS, hc * dk2)
    s_t = mm(kbd_t, q_t, _pick(kbd_t.shape[0], (512, 256, 128)), _pick(t, (512, 256, 128)))
    rank2, e2z, nsel, e1 = peer_select(s_t)
    out_t = peer_dense(xn_t, u_tab.astype(BF16), v_tab.T.astype(BF16), rank2, e2z, nsel, e1,
                       tm=_pick(t, (512, 256, 128)))
    return out_t.T


def _fox_norm_kernel(p_ref, gq_ref, gk_ref, q_ref, k_ref):
    scale = FOX_DH ** -0.5
    for h in range(FOX_HEADS):
        lo, hi = h * FOX_DH, (h + 1) * FOX_DH
        x = p_ref[:, lo:hi]
        y = x * lax.rsqrt(jnp.mean(x * x, axis=-1, keepdims=True) + RMS_EPS) * gq_ref[...]
        q_ref[:, lo:hi] = (y * scale).astype(q_ref.dtype)
        x = p_ref[:, FOX_W + lo:FOX_W + hi]
        k_ref[:, lo:hi] = x * lax.rsqrt(jnp.mean(x * x, axis=-1, keepdims=True) + RMS_EPS) * gk_ref[...]


def fox_norm(p, g_q, g_k):
    t = p.shape[0]
    tm = _pick(t, (256, 128))
    return pl.pallas_call(
        _fox_norm_kernel,
        grid=(t // tm,),
        in_specs=[pl.BlockSpec((tm, 2 * FOX_W), lambda i: (i, 0)),
                  pl.BlockSpec((1, FOX_DH), lambda i: (0, 0)), pl.BlockSpec((1, FOX_DH), lambda i: (0, 0))],
        out_specs=[pl.BlockSpec((tm, FOX_W), lambda i: (i, 0))] * 2,
        out_shape=[jax.ShapeDtypeStruct((t, FOX_W), BF16), jax.ShapeDtypeStruct((t, FOX_W), F32)],
        compiler_params=_params("parallel"),
        name="fox_norm",
    )(p, g_q.reshape(1, FOX_DH), g_k.reshape(1, FOX_DH))


def _fox_prompt_kernel(qi_ref, ki_ref, q_ref, k_ref, v_ref, f_ref, o_ref, m_ref, l_ref, acc_ref, *, tq, tk):
    n = pl.program_id(2)
    i = qi_ref[n]
    kk = ki_ref[n]

    @pl.when(kk == 0)
    def _():
        m_ref[...] = jnp.full_like(m_ref, NEG_INF)
        l_ref[...] = jnp.zeros_like(l_ref)
        acc_ref[...] = jnp.zeros_like(acc_ref)

    def step(masked):
        s = lax.dot_general(q_ref[...], k_ref[...].astype(BF16), (((1,), (1,)), ((), ())),
                            preferred_element_type=F32)
        s = s - f_ref[0]
        if masked:
            q_pos = i * tq + lax.broadcasted_iota(jnp.int32, s.shape, 0)
            k_pos = kk * tk + lax.broadcasted_iota(jnp.int32, s.shape, 1)
            s = jnp.where(q_pos >= k_pos, s, NEG_INF)
        m_prev = m_ref[...]
        m_new = jnp.maximum(m_prev, jnp.max(s, axis=1, keepdims=True))
        alpha = jnp.exp(m_prev - m_new)
        p = jnp.exp(s - m_new)
        l_ref[...] = alpha * l_ref[...] + jnp.sum(p, axis=1, keepdims=True)
        acc_ref[...] = alpha * acc_ref[...] + jnp.dot(p.astype(BF16), v_ref[...].astype(BF16),
                                                      preferred_element_type=F32)
        m_ref[...] = m_new

    on_diag = (kk + 1) * tk > i * tq + 1

    @pl.when(on_diag)
    def _():
        step(True)

    @pl.when(jnp.logical_not(on_diag))
    def _():
        step(False)

    @pl.when(kk == (i * tq + tq - 1) // tk)
    def _():
        o_ref[...] = (acc_ref[...] / l_ref[...]).astype(o_ref.dtype)


def fox_prompt(q_bf, k, p, f_cum, bsz, s_len, tq=1024, tk=512):
    nq, nk = s_len // tq, s_len // tk
    pairs = [(i, kk) for i in range(nq) for kk in range((i * tq + tq - 1) // tk + 1)]
    qi = jnp.asarray([a for a, _ in pairs], jnp.int32)
    ki = jnp.asarray([b for _, b in pairs], jnp.int32)
    v_col0 = 2 * FOX_W // FOX_DH
    grid_spec = pltpu.PrefetchScalarGridSpec(
        num_scalar_prefetch=2,
        grid=(bsz, FOX_HEADS, len(pairs)),
        in_specs=[pl.BlockSpec((tq, FOX_DH), lambda b, h, n, qi, ki: (b * nq + qi[n], h)),
                  pl.BlockSpec((tk, FOX_DH), lambda b, h, n, qi, ki: (b * nk + ki[n], h)),
                  pl.BlockSpec((tk, FOX_DH), lambda b, h, n, qi, ki: (b * nk + ki[n], v_col0 + h)),
                  pl.BlockSpec((1, 1, tk), lambda b, h, n, qi, ki: (b * FOX_HEADS + h, 0, ki[n]))],
        out_specs=pl.BlockSpec((tq, FOX_DH), lambda b, h, n, qi, ki: (b * nq + qi[n], h)),
        scratch_shapes=[pltpu.VMEM((tq, 1), F32), pltpu.VMEM((tq, 1), F32), pltpu.VMEM((tq, FOX_DH), F32)],
    )
    return pl.pallas_call(
        functools.partial(_fox_prompt_kernel, tq=tq, tk=tk),
        grid_spec=grid_spec,
        out_shape=jax.ShapeDtypeStruct((bsz * s_len, FOX_W), BF16),
        compiler_params=_params("parallel", "parallel", "arbitrary"),
        name="fox_prompt",
    )(qi, ki, q_bf, k, p, f_cum)


SUB = 8


def _logf_cumsum_kernel(pt_ref, *refs):
    lf_refs, o_ref = refs[:-1], refs[-1]
    page, nh = lf_refs[0].shape[1:]
    tri = (lax.broadcasted_iota(jnp.int32, (page, page), 1)
           <= lax.broadcasted_iota(jnp.int32, (page, page), 0)).astype(F32)
    carry = jnp.zeros((1, nh), F32)
    for p, lf_ref in enumerate(lf_refs):
        f = jnp.dot(tri, lf_ref[0], precision=lax.Precision.HIGHEST, preferred_element_type=F32) + carry
        carry = f[page - 1:page, :]
        o_ref[0, p] = f


def paged_logf_cumsum(cache_logf, page_table):
    bsz, n_pages = page_table.shape
    _, page, nh = cache_logf.shape

    def page_spec(p):
        return pl.BlockSpec((1, page, nh), lambda b, pt: (pt[b, p], 0, 0))

    grid_spec = pltpu.PrefetchScalarGridSpec(
        num_scalar_prefetch=1,
        grid=(bsz,),
        in_specs=[page_spec(p) for p in range(n_pages)],
        out_specs=pl.BlockSpec((1, n_pages, page, nh), lambda b, pt: (b, 0, 0, 0)),
    )
    return pl.pallas_call(
        _logf_cumsum_kernel,
        grid_spec=grid_spec,
        out_shape=jax.ShapeDtypeStruct((bsz, n_pages, page, nh), F32),
        compiler_params=_params("parallel"),
        name="logf_cumsum",
    )(page_table, *([cache_logf] * n_pages))


def _fox_sample_kernel(pt_ref, q_ref, *refs, n_groups, group):
    kc_refs, vc_refs = refs[:group], refs[group:2 * group]
    f_ref, mask_ref, kn_ref, vn_ref, fn_ref, maskn_ref, o_ref, m_ref, l_ref, acc_ref = refs[2 * group:]
    pg = pl.program_id(1)

    @pl.when(pg == 0)
    def _():
        m_ref[...] = jnp.full_like(m_ref, NEG_INF)
        l_ref[...] = jnp.zeros_like(l_ref)
        acc_ref[...] = jnp.zeros_like(acc_ref)

    def attend(kvb):
        ss = [lax.dot_general(q_ref[0], k2, (((1,), (1,)), ((), ())), preferred_element_type=F32) + b
              for k2, _, b in kvb]
        m_prev = m_ref[...]
        m_new = m_prev
        for s in ss:
            m_new = jnp.maximum(m_new, jnp.max(s, axis=1, keepdims=True))
        alpha = jnp.exp(m_prev - m_new)
        l_new = alpha * l_ref[...]
        acc = alpha * acc_ref[...]
        for s, (_, v2, _) in zip(ss, kvb):
            p = jnp.exp(s - m_new)
            l_new = l_new + jnp.sum(p, axis=1, keepdims=True)
            acc = acc + jnp.dot(p.astype(BF16), v2, preferred_element_type=F32)
        l_ref[...] = l_new
        acc_ref[...] = acc
        m_ref[...] = m_new

    @pl.when(pg < n_groups)
    def _():
        rows = kc_refs[0].shape[0] * kc_refs[0].shape[1]
        attend([(kc[...].reshape(rows, FOX_DH).astype(BF16), vc[...].reshape(rows, FOX_DH).astype(BF16),
                 mask_ref[...] - f_ref[0, g])
                for g, (kc, vc) in enumerate(zip(kc_refs, vc_refs))])

    @pl.when(pg == n_groups)
    def _():
        attend([(kn_ref[0].astype(BF16), vn_ref[0].astype(BF16), maskn_ref[...] - fn_ref[0])])
        o_ref[0] = acc_ref[...] / l_ref[...]


def fox_sample(q_bf, k_new, v_new, lf_new, cache_k, cache_v, cache_logf, page_table):
    bsz, t_len, _ = q_bf.shape
    n_pool, page = cache_k.shape[:2]
    n_pages = page_table.shape[1]
    nh, dh = FOX_HEADS, FOX_DH
    rows = nh * t_len
    lanes = page * nh
    lanes_new = SUB * nh
    qf = q_bf.reshape(bsz, t_len, nh, dh).transpose(0, 2, 1, 3).reshape(bsz, rows, dh)
    f_pages = paged_logf_cumsum(cache_logf, page_table)
    f_rows = f_pages.reshape(bsz, n_pages, 1, lanes)
    pad = ((0, 0), (0, SUB - t_len), (0, 0))
    kn = jnp.pad(k_new, pad).reshape(bsz, lanes_new, dh)
    vn = jnp.pad(v_new, pad).reshape(bsz, lanes_new, dh)
    fn = (f_pages[:, -1, -1, :][:, None, :] + jnp.cumsum(jnp.pad(lf_new, pad), axis=1)).reshape(bsz, 1, lanes_new)
    row_head, row_q = np.arange(rows) // t_len, np.arange(rows) % t_len
    mask = np.where(row_head[:, None] == (np.arange(lanes) % nh)[None, :], 0.0, NEG_INF).astype(np.float32)
    key_n, head_n = np.arange(lanes_new) // nh, np.arange(lanes_new) % nh
    ok = (row_head[:, None] == head_n[None, :]) & (key_n[None, :] <= row_q[:, None]) & (key_n[None, :] < t_len)
    mask_new = np.where(ok, 0.0, NEG_INF).astype(np.float32)
    group = _pick(n_pages, (4, 2, 1))
    n_groups = n_pages // group
    last = n_groups - 1

    def cache_spec(g):
        return pl.BlockSpec((page, nh, dh), lambda b, p, pt: (pt[b, jnp.minimum(p, last) * group + g], 0, 0))

    cache_specs = [cache_spec(g) for g in range(group)]
    kc3 = cache_k.reshape(n_pool * page, nh, dh)
    vc3 = cache_v.reshape(n_pool * page, nh, dh)
    grid_spec = pltpu.PrefetchScalarGridSpec(
        num_scalar_prefetch=1,
        grid=(bsz, n_groups + 1),
        in_specs=[pl.BlockSpec((1, rows, dh), lambda b, p, pt: (b, 0, 0)),
                  *cache_specs, *cache_specs,
                  pl.BlockSpec((1, group, 1, lanes), lambda b, p, pt: (b, jnp.minimum(p, last), 0, 0)),
                  pl.BlockSpec((rows, lanes), lambda b, p, pt: (0, 0)),
                  pl.BlockSpec((1, lanes_new, dh), lambda b, p, pt: (b, 0, 0)),
                  pl.BlockSpec((1, lanes_new, dh), lambda b, p, pt: (b, 0, 0)),
                  pl.BlockSpec((1, 1, lanes_new), lambda b, p, pt: (b, 0, 0)),
                  pl.BlockSpec((rows, lanes_new), lambda b, p, pt: (0, 0))],
        out_specs=pl.BlockSpec((1, rows, dh), lambda b, p, pt: (b, 0, 0)),
        scratch_shapes=[pltpu.VMEM((rows, 1), F32), pltpu.VMEM((rows, 1), F32), pltpu.VMEM((rows, dh), F32)],
    )
    out = pl.pallas_call(
        functools.partial(_fox_sample_kernel, n_groups=n_groups, group=group),
        grid_spec=grid_spec,
        out_shape=jax.ShapeDtypeStruct((bsz, rows, dh), F32),
        compiler_params=_params("parallel", "arbitrary"),
        name="fox_sample",
    )(page_table, qf, *([kc3] * group), *([vc3] * group),
      f_rows, jnp.asarray(mask), kn, vn, fn, jnp.asarray(mask_new))
    return out.reshape(bsz, nh, t_len, dh).transpose(0, 2, 1, 3).reshape(bsz, t_len, nh * dh)


def _mlstm_kernel(q_ref, k_ref, v_ref, o_ref, g_ref, c0_ref, n0_ref, m0_ref, mg_ref,
                  h_ref, c1_ref, n1_ref, m1_ref, c_scr, n_scr, m_scr):
    c = pl.program_id(1)
    L = q_ref.shape[0]
    nh, dh = MLSTM_HEADS, MLSTM_DH

    @pl.when(c == 0)
    def _():
        c_scr[...] = c0_ref[0]
        n_scr[...] = n0_ref[0]
        m_scr[...] = m0_ref[0]

    t_idx = lax.broadcasted_iota(jnp.int32, (L, L), 0)
    s_idx = lax.broadcasted_iota(jnp.int32, (L, L), 1)
    causal = s_idx <= t_idx
    diag = s_idx == t_idx
    for h in range(nh):
        cols = slice(h * dh, (h + 1) * dh)
        ig = g_ref[0, h:h + 1, :]
        lf = g_ref[0, nh + h:nh + h + 1, :]
        m0 = m_scr[h:h + 1, :]
        f_col = jnp.sum(jnp.where(causal, lf, 0.0), axis=1, keepdims=True)
        ig_col = jnp.sum(jnp.where(diag, ig, 0.0), axis=1, keepdims=True)
        g_col = ig_col - f_col
        g_row = jnp.sum(jnp.where(diag, g_col, 0.0), axis=0, keepdims=True)
        cm_col = jnp.max(jnp.where(causal, g_row, NEG_INF), axis=1, keepdims=True)
        m_col = f_col + jnp.maximum(m0, cm_col)
        b_col = f_col - m_col
        a_col = jnp.exp(f_col + m0 - m_col)
        d = jnp.exp(jnp.where(causal, b_col + g_row, NEG_INF))
        q = q_ref[:, cols] * (dh ** -0.5)
        k = k_ref[:, cols]
        v = v_ref[:, cols]
        qb, kb, vb = q.astype(BF16), k.astype(BF16), v.astype(BF16)
        qk = lax.dot_general(qb, kb, (((1,), (1,)), ((), ())), preferred_element_type=F32) * d
        c_old = c_scr[h]
        n_old = n_scr[h:h + 1, :]
        num = a_col * jnp.dot(qb, c_old.astype(BF16), preferred_element_type=F32) \
            + jnp.dot(qk.astype(BF16), vb, preferred_element_type=F32)
        den = a_col * jnp.sum(q * n_old, axis=1, keepdims=True) + jnp.sum(qk, axis=1, keepdims=True)
        hs = num / jnp.maximum(jnp.abs(den), jnp.exp(-m_col))
        hn = hs * lax.rsqrt(jnp.mean(hs * hs, axis=1, keepdims=True) + RMS_EPS) * mg_ref[:, cols]
        h_ref[:, cols] = (hn * jax.nn.sigmoid(o_ref[:, cols])).astype(h_ref.dtype)
        b_last = b_col[L - 1:L, :]
        a_last = a_col[L - 1:L, :]
        kd = k * jnp.exp(b_last + g_col)
        c_scr[h] = a_last * c_old + lax.dot_general(kd.astype(BF16), vb, (((0,), (0,)), ((), ())),
                                                     preferred_element_type=F32)
        n_scr[h:h + 1, :] = a_last * n_old + jnp.sum(kd, axis=0, keepdims=True)
        m_scr[h:h + 1, :] = m_col[L - 1:L, :]

    @pl.when(c == pl.num_programs(1) - 1)
    def _():
        c1_ref[0] = c_scr[...]
        n1_ref[0] = n_scr[...]
        m1_ref[0] = m_scr[...]


def mlstm(p, gates, c0, n0, m0, mh_g, bsz, n_chunks, chunk, col0):
    nh, dh, mw = MLSTM_HEADS, MLSTM_DH, MLSTM_W
    rows = bsz * n_chunks * chunk

    def col_spec(j):
        return pl.BlockSpec((chunk, mw), lambda b, c: (b * n_chunks + c, col0 + j))

    state = lambda b, c: (b, 0, 0)
    return pl.pallas_call(
        _mlstm_kernel,
        grid=(bsz, n_chunks),
        in_specs=[col_spec(0), col_spec(1), col_spec(2), col_spec(3),
                  pl.BlockSpec((1, 2 * nh, chunk), lambda b, c: (b * n_chunks + c, 0, 0)),
                  pl.BlockSpec((1, nh, dh, dh), lambda b, c: (b, 0, 0, 0)),
                  pl.BlockSpec((1, nh, dh), state), pl.BlockSpec((1, nh, 1), state),
                  pl.BlockSpec((1, mw), lambda b, c: (0, 0))],
        out_specs=[pl.BlockSpec((chunk, mw), lambda b, c: (b * n_chunks + c, 0)),
                   pl.BlockSpec((1, nh, dh, dh), lambda b, c: (b, 0, 0, 0)),
                   pl.BlockSpec((1, nh, dh), state), pl.BlockSpec((1, nh, 1), state)],
        out_shape=[jax.ShapeDtypeStruct((rows, mw), BF16),
                   jax.ShapeDtypeStruct((bsz, nh, dh, dh), F32),
                   jax.ShapeDtypeStruct((bsz, nh, dh), F32),
                   jax.ShapeDtypeStruct((bsz, nh, 1), F32)],
        scratch_shapes=[pltpu.VMEM((nh, dh, dh), F32), pltpu.VMEM((nh, dh), F32), pltpu.VMEM((nh, 1), F32)],
        compiler_params=_params("parallel", "arbitrary"),
        name="mlstm",
    )(p, p, p, p, gates, c0, n0, m0.reshape(bsz, nh, 1), mh_g.reshape(1, mw))


CONV_HALO = 32
CONV_ROWS = 32


def _group_ln_swish(y, g, b):
    outs = []
    for grp in range(CONV_GROUPS):
        x = y[:, grp * LANES:(grp + 1) * LANES]
        mu = jnp.mean(x, axis=1, keepdims=True)
        xc = x - mu
        var = jnp.mean(xc * xc, axis=1, keepdims=True)
        z = xc * lax.rsqrt(var + LN_EPS) * g[:, grp * LANES:(grp + 1) * LANES] + b[:, grp * LANES:(grp + 1) * LANES]
        outs.append(z * jax.nn.sigmoid(z))
    return jnp.concatenate(outs, axis=1)


def _conv_prompt_kernel(cv_ref, cg_ref, hv_ref, hg_ref, w_ref, b_ref, g_ref, beta_ref, y_ref, buf_ref, ext_ref):
    i = pl.program_id(1)
    ts = cv_ref.shape[0]
    hist = jnp.where(i > 0, hv_ref[...] * jax.nn.sigmoid(hg_ref[...]), 0.0)
    ext_ref[0:CONV_HALO, :] = hist
    ext_ref[CONV_HALO:, :] = cv_ref[...] * jax.nn.sigmoid(cg_ref[...])
    off = CONV_HALO - (CONV_K - 1)
    for r0 in range(0, ts, CONV_ROWS):
        acc = jnp.zeros((CONV_ROWS, CONV_W), F32) + b_ref[...]
        for j in range(CONV_K):
            acc = acc + ext_ref[r0 + off + j:r0 + off + j + CONV_ROWS, :] * w_ref[j:j + 1, :]
        y_ref[r0:r0 + CONV_ROWS, :] = _group_ln_swish(acc, g_ref[...], beta_ref[...]).astype(y_ref.dtype)

    @pl.when(i == pl.num_programs(1) - 1)
    def _():
        buf_ref[0] = ext_ref[ts + off:ts + CONV_HALO, :]


def conv_prompt(p, conv_w, conv_b, cn_g, cn_b, bsz, s_len, ts=512):
    nt = s_len // ts
    hb = ts // CONV_HALO

    def halo(col):
        return pl.BlockSpec((CONV_HALO, CONV_W), lambda b, i: (jnp.maximum((b * nt + i) * hb - 1, 0), col))

    vec = pl.BlockSpec((1, CONV_W), lambda b, i: (0, 0))
    return pl.pallas_call(
        _conv_prompt_kernel,
        grid=(bsz, nt),
        in_specs=[pl.BlockSpec((ts, CONV_W), lambda b, i: (b * nt + i, 0)),
                  pl.BlockSpec((ts, CONV_W), lambda b, i: (b * nt + i, 1)),
                  halo(0), halo(1),
                  pl.BlockSpec((CONV_K, CONV_W), lambda b, i: (0, 0)), vec, vec, vec],
        out_specs=[pl.BlockSpec((ts, CONV_W), lambda b, i: (b * nt + i, 0)),
                   pl.BlockSpec((1, CONV_K - 1, CONV_W), lambda b, i: (b, 0, 0))],
        out_shape=[jax.ShapeDtypeStruct((bsz * s_len, CONV_W), BF16),
                   jax.ShapeDtypeStruct((bsz, CONV_K - 1, CONV_W), F32)],
        scratch_shapes=[pltpu.VMEM((ts + CONV_HALO, CONV_W), F32)],
        compiler_params=_params("parallel", "arbitrary"),
        name="conv_prompt",
    )(p, p, p, p, conv_w, conv_b.reshape(1, -1), cn_g.reshape(1, -1), cn_b.reshape(1, -1))


def _conv_sample_kernel(cv_ref, cg_ref, st_ref, w_ref, b_ref, g_ref, beta_ref, y_ref, buf_ref, ext_ref, *, t_len):
    nb = st_ref.shape[0]
    hist = CONV_K - 1
    y_ref[...] = jnp.zeros_like(y_ref)
    for bb in range(nb):
        rows = slice(bb * SUB, bb * SUB + t_len)
        ext_ref[0:hist, :] = st_ref[bb]
        ext_ref[hist:hist + t_len, :] = cv_ref[rows, :] * jax.nn.sigmoid(cg_ref[rows, :])
        ys = [jnp.sum(ext_ref[t:t + CONV_K, :] * w_ref[...], axis=0, keepdims=True) for t in range(t_len)]
        y = jnp.concatenate(ys, axis=0) + b_ref[...]
        y_ref[rows, :] = _group_ln_swish(y, g_ref[...], beta_ref[...]).astype(y_ref.dtype)
        buf_ref[bb] = ext_ref[t_len:t_len + hist, :]


def conv_sample(p_pad, state_conv, conv_w, conv_b, cn_g, cn_b, t_len, nb=8):
    bsz = state_conv.shape[0]
    vec = pl.BlockSpec((1, CONV_W), lambda i: (0, 0))
    return pl.pallas_call(
        functools.partial(_conv_sample_kernel, t_len=t_len),
        grid=(bsz // nb,),
        in_specs=[pl.BlockSpec((nb * SUB, CONV_W), lambda i: (i, 0)),
                  pl.BlockSpec((nb * SUB, CONV_W), lambda i: (i, 1)),
                  pl.BlockSpec((nb, CONV_K - 1, CONV_W), lambda i: (i, 0, 0)),
                  pl.BlockSpec((CONV_K, CONV_W), lambda i: (0, 0)), vec, vec, vec],
        out_specs=[pl.BlockSpec((nb * SUB, CONV_W), lambda i: (i, 0)),
                   pl.BlockSpec((nb, CONV_K - 1, CONV_W), lambda i: (i, 0, 0))],
        out_shape=[jax.ShapeDtypeStruct((bsz * SUB, CONV_W), BF16),
                   jax.ShapeDtypeStruct((bsz, CONV_K - 1, CONV_W), F32)],
        scratch_shapes=[pltpu.VMEM((CONV_K - 1 + SUB, CONV_W), F32)],
        compiler_params=_params("parallel"),
        name="conv_sample",
    )(p_pad, p_pad, state_conv, conv_w, conv_b.reshape(1, -1), cn_g.reshape(1, -1), cn_b.reshape(1, -1))


def kernel(x_prompt, x_sample, state_conv, state_C, state_n, state_m, cache_k, cache_v, cache_logf, page_table,
           norm0_mix, w_in0, b_ig0, b_fg0, conv_w0, conv_b0, conv_norm_g0, conv_norm_b0, mlstm_norm_g0, w_out0,
           norm1_mix, w_in1, b_f1, q_norm_g1, k_norm_g1, w_out1,
           norm0_ffn, peer_wq0, peer_keys0, peer_u0, peer_v0,
           norm1_ffn, peer_wq1, peer_keys1, peer_u1, peer_v1):
    bp, sp, d = x_prompt.shape
    bs, ss, _ = x_sample.shape
    tp, ts = bp * sp, bs * ss
    x = jnp.concatenate([x_prompt.reshape(tp, d), x_sample.reshape(ts, d)], axis=0)

    n_main0 = 2 * CONV_W + 4 * MLSTM_W
    xn = rmsnorm_bf16(x, norm0_mix)
    p0 = mm_tokens(xn, w_in0[:, :n_main0].astype(BF16))
    g0 = gate_proj(x, norm0_mix, w_in0[:, n_main0:])
    nh = MLSTM_HEADS
    gates = jnp.concatenate([g0[:, :nh] + b_ig0, jax.nn.log_sigmoid(g0[:, nh:] + b_fg0)], axis=1)
    conv_par = (conv_w0, conv_b0, conv_norm_g0, conv_norm_b0)
    mcol0 = 2 * CONV_W // MLSTM_W
    nc = sp // MLSTM_CHUNK
    gates_p = gates[:tp].reshape(bp * nc, MLSTM_CHUNK, 2 * nh).transpose(0, 2, 1)
    hp, p_C, p_n, p_m = mlstm(p0, gates_p, jnp.zeros((bp, nh, MLSTM_DH, MLSTM_DH), F32),
                              jnp.zeros((bp, nh, MLSTM_DH), F32), jnp.zeros((bp, nh), F32),
                              mlstm_norm_g0, bp, nc, MLSTM_CHUNK, mcol0)
    yp, p_conv = conv_prompt(p0, *conv_par, bp, sp)
    tpad = ((0, 0), (0, SUB - ss), (0, 0))
    p0s = jnp.pad(p0[tp:].reshape(bs, ss, -1), tpad).reshape(bs * SUB, -1)
    gs = gates[tp:].reshape(bs, ss, 2 * nh)
    gates_s = jnp.concatenate([jnp.pad(gs[..., :nh], tpad, constant_values=NEG_INF),
                               jnp.pad(gs[..., nh:], tpad)], axis=-1).transpose(0, 2, 1)
    hs, s_C, s_n, s_m = mlstm(p0s, gates_s, state_C, state_n, state_m, mlstm_norm_g0, bs, 1, SUB, mcol0)
    ys, s_conv = conv_sample(p0s, state_conv, *conv_par, ss)
    unpad = lambda a: a.reshape(bs, SUB, -1)[:, :ss].reshape(ts, -1)
    y_all = jnp.concatenate([yp, unpad(ys)], axis=0)
    h_all = jnp.concatenate([hp[:tp], unpad(hs)], axis=0)
    w_out0_bf = w_out0.astype(BF16)
    x = x + mm2(y_all, h_all, w_out0_bf[:CONV_W], w_out0_bf[CONV_W:], _pick(tp + ts, (512, 256, 128)), 1024)
    p_m, s_m = p_m.reshape(bp, nh), s_m.reshape(bs, nh)
    x = x + peer(x, norm0_ffn, peer_wq0, peer_keys0, peer_u0, peer_v0)

    xn = rmsnorm_bf16(x, norm1_mix)
    p1 = mm_tokens(xn, w_in1[:, :3 * FOX_W].astype(BF16))
    g1 = gate_proj(x, norm1_mix, w_in1[:, 3 * FOX_W:])
    lf = jax.nn.log_sigmoid(g1 + b_f1)
    q_bf, k = fox_norm(p1, q_norm_g1, k_norm_g1)
    v = p1[:, 2 * FOX_W:]
    f_cum = jnp.cumsum(lf[:tp].reshape(bp, sp, FOX_HEADS), axis=1).transpose(0, 2, 1).reshape(bp * FOX_HEADS, 1, sp)
    attn_p = fox_prompt(q_bf, k, p1, f_cum, bp, sp)
    attn_s = fox_sample(q_bf[tp:].reshape(bs, ss, FOX_W), k[tp:].reshape(bs, ss, FOX_W),
                        v[tp:].reshape(bs, ss, FOX_W), lf[tp:].reshape(bs, ss, FOX_HEADS),
                        cache_k, cache_v, cache_logf, page_table)
    attn = jnp.concatenate([attn_p, attn_s.reshape(ts, FOX_W).astype(BF16)], axis=0)
    x = x + mm_tokens(attn, w_out1.astype(BF16))
    x = x + peer(x, norm1_ffn, peer_wq1, peer_keys1, peer_u1, peer_v1)

    hd = (FOX_HEADS, FOX_DH)
    return (x[:tp].reshape(bp, sp, d), x[tp:].reshape(bs, ss, d), p_conv, p_C, p_n, p_m,
            k[:tp].reshape(bp, sp, *hd), v[:tp].reshape(bp, sp, *hd), lf[:tp].reshape(bp, sp, FOX_HEADS),
            s_conv, s_C, s_n, s_m,
            k[tp:].reshape(bs, ss, *hd), v[tp:].reshape(bs, ss, *hd), lf[tp:].reshape(bs, ss, FOX_HEADS))
```

```python
import functools
import math

import jax
import jax.numpy as jnp
import numpy as np
from jax import lax
from jax.experimental import pallas as pl
from jax.experimental.pallas import tpu as pltpu

F32 = jnp.float32
BF16 = jnp.bfloat16

D_MODEL = 2048
CONV_W = 1024
CONV_K = 31
CONV_GROUPS = 8
MLSTM_HEADS = 4
MLSTM_W = 1024
MLSTM_DH = 256
MLSTM_CHUNK = 128
FOX_HEADS = 16
FOX_DH = 128
FOX_W = 2048
PEER_HEADS = 8
PEER_NKEYS = 128
PEER_DK = 128
PEER_TOPK = 16
RMS_EPS = 1e-6
LN_EPS = 1e-5

LANES = 128
VMEM_LIMIT = 56 * 1024 * 1024
NEG_INF = float("-inf")


def _params(*sem):
    return pltpu.CompilerParams(dimension_semantics=sem, vmem_limit_bytes=VMEM_LIMIT)


def _rmsnorm_kernel(x_ref, g_ref, o_ref):
    x = x_ref[...]
    y = x * lax.rsqrt(jnp.mean(x * x, axis=-1, keepdims=True) + RMS_EPS)
    o_ref[...] = (y * g_ref[...]).astype(o_ref.dtype)


def rmsnorm_bf16(x, g):
    t, d = x.shape
    tm = _pick(t, (512, 256, 128))
    return pl.pallas_call(
        _rmsnorm_kernel,
        grid=(t // tm,),
        in_specs=[pl.BlockSpec((tm, d), lambda i: (i, 0)), pl.BlockSpec((1, d), lambda i: (0, 0))],
        out_specs=pl.BlockSpec((tm, d), lambda i: (i, 0)),
        out_shape=jax.ShapeDtypeStruct((t, d), BF16),
        compiler_params=_params("parallel"),
        name="rmsnorm",
    )(x, g.reshape(1, d))


def _gate_proj_kernel(x_ref, g_ref, w_ref, o_ref):
    x = x_ref[...]
    y = x * lax.rsqrt(jnp.mean(x * x, axis=-1, keepdims=True) + RMS_EPS) * g_ref[...]
    o_ref[...] = jnp.dot(y, w_ref[...], precision=lax.Precision.HIGHEST, preferred_element_type=F32)


def gate_proj(x, g, w):
    t, d = x.shape
    n = w.shape[1]
    tm = _pick(t, (256, 128))
    out = pl.pallas_call(
        _gate_proj_kernel,
        grid=(t // tm,),
        in_specs=[pl.BlockSpec((tm, d), lambda i: (i, 0)), pl.BlockSpec((1, d), lambda i: (0, 0)),
                  pl.BlockSpec((d, LANES), lambda i: (0, 0))],
        out_specs=pl.BlockSpec((tm, LANES), lambda i: (i, 0)),
        out_shape=jax.ShapeDtypeStruct((t, LANES), F32),
        compiler_params=_params("parallel"),
        name="gate_proj",
    )(x, g.reshape(1, d), jnp.pad(w, ((0, 0), (0, LANES - n))))
    return out[:, :n]


def _mm_kernel(x_ref, w_ref, o_ref):
    o_ref[...] = jnp.dot(x_ref[...], w_ref[...], preferred_element_type=F32).astype(o_ref.dtype)


def mm(x, w, tm, tn, out_dtype=F32):
    m, k = x.shape
    n = w.shape[1]
    assert m % tm == 0 and n % tn == 0, (m, n, tm, tn)
    return pl.pallas_call(
        _mm_kernel,
        grid=(n // tn, m // tm),
        in_specs=[pl.BlockSpec((tm, k), lambda j, i: (i, 0)), pl.BlockSpec((k, tn), lambda j, i: (0, j))],
        out_specs=pl.BlockSpec((tm, tn), lambda j, i: (i, j)),
        out_shape=jax.ShapeDtypeStruct((m, n), out_dtype),
        compiler_params=_params("parallel", "parallel"),
        name="mm",
    )(x, w)


def _mm2_kernel(x1_ref, x2_ref, w1_ref, w2_ref, o_ref):
    o_ref[...] = (jnp.dot(x1_ref[...], w1_ref[...], preferred_element_type=F32)
                  + jnp.dot(x2_ref[...], w2_ref[...], preferred_element_type=F32))


def mm2(x1, x2, w1, w2, tm, tn):
    m, k1 = x1.shape
    k2 = x2.shape[1]
    n = w1.shape[1]
    return pl.pallas_call(
        _mm2_kernel,
        grid=(n // tn, m // tm),
        in_specs=[pl.BlockSpec((tm, k1), lambda j, i: (i, 0)), pl.BlockSpec((tm, k2), lambda j, i: (i, 0)),
                  pl.BlockSpec((k1, tn), lambda j, i: (0, j)), pl.BlockSpec((k2, tn), lambda j, i: (0, j))],
        out_specs=pl.BlockSpec((tm, tn), lambda j, i: (i, j)),
        out_shape=jax.ShapeDtypeStruct((m, n), F32),
        compiler_params=_params("parallel", "parallel"),
        name="mm2",
    )(x1, x2, w1, w2)


def _pick(n, cands):
    for c in cands:
        if n % c == 0:
            return c
    raise ValueError(n)


def mm_tokens(x, w, out_dtype=F32):
    t = x.shape[0]
    n = w.shape[1]
    return mm(x, w, _pick(t, (512, 256, 128)), _pick(n, (1024, 512, 256, 128)), out_dtype)


def _topk_rows(arrays, k):
    n, lanes = arrays[0].shape
    iota = lax.broadcasted_iota(jnp.int32, (n, lanes), 0)
    iota_k = lax.broadcasted_iota(jnp.int32, (k, lanes), 0)
    unranked = jnp.full((n, lanes), k, jnp.int32)
    zeros_k = jnp.zeros((k, lanes), F32)

    def run(exact_ties):
        carries_rank = [exact_ties or i == len(arrays) - 1 for i in range(len(arrays))]

        def body(r, carry):
            out = []
            for (w, rank, vals), has_rank in zip(carry, carries_rank):
                mx = jnp.max(w, axis=0, keepdims=True)
                hit = w == mx
                if exact_ties:
                    hit = iota == jnp.min(jnp.where(hit, iota, n), axis=0, keepdims=True)
                out.append((jnp.where(hit, NEG_INF, w), jnp.where(hit, r, rank) if has_rank else rank,
                            jnp.where(iota_k == r, mx, vals)))
            return tuple(out)

        init = tuple((s, unranked if has_rank else 0, zeros_k) for s, has_rank in zip(arrays, carries_rank))
        res = []
        for s, (_, rank, vals), has_rank in zip(arrays, lax.fori_loop(0, k, body, init), carries_rank):
            if not has_rank:
                rank = unranked
                for r in range(k):
                    rank = jnp.where(s == vals[r:r + 1, :], r, rank)
            res.append((rank, vals))
        return tuple(res)

    fast = run(False)
    ranked = sum(jnp.sum((rank < k).astype(jnp.int32), axis=0, keepdims=True) for rank, _ in fast)
    clean = jnp.max(ranked) == k * len(arrays)
    res = lax.cond(clean, lambda: fast, lambda: run(True))
    return [(vals, rank) for rank, vals in res]


def _peer_select_kernel(s_ref, rank2_ref, e2z_ref, nsel_ref, e1_ref):
    k = PEER_TOPK
    half = k // 2
    L = s_ref.shape[1]
    sub = lax.broadcasted_iota(jnp.int32, (half, L), 0)

    def head(h, _):
        base = pl.multiple_of(h * (2 * PEER_NKEYS), 2 * PEER_NKEYS)
        s1 = s_ref[pl.ds(base, PEER_NKEYS), :]
        s2 = s_ref[pl.ds(base + PEER_NKEYS, PEER_NKEYS), :]
        (v1, rank1), (v2, rank2) = _topk_rows([s1, s2], k)
        pieces, pos = [], []
        for r1 in range(half):
            pieces.append(v1[r1:r1 + 1, :] + v2[0:half, :])
            pos.append(sub + r1 * k)
        pieces.append(v1[0:1, :] + v2[half:k, :])
        pos.append(sub + half)
        pieces.append(v1[half:k, :] + v2[0:1, :])
        pos.append((sub + half) * k)
        cand = jnp.concatenate(pieces, axis=0)
        posid = jnp.concatenate(pos, axis=0)
        big = k * k

        def pick_all(exact_ties):
            def pick(_, w):
                mx = jnp.max(w, axis=0, keepdims=True)
                hit = w == mx
                if exact_ties:
                    hit = posid == jnp.min(jnp.where(hit, posid, big), axis=0, keepdims=True)
                return jnp.where(hit, NEG_INF, w)

            return (lax.fori_loop(0, k, pick, cand) == NEG_INF).astype(F32)

        sel_fast = pick_all(False)
        picked = jnp.sum(sel_fast, axis=0, keepdims=True)
        sel = lax.cond(jnp.max(picked) == k, lambda: sel_fast, lambda: pick_all(True))
        top = v1[0:1, :] + v2[0:1, :]
        z = jnp.sum(sel * jnp.exp(cand - top), axis=0, keepdims=True)
        counts = []
        for r1 in range(half):
            c = jnp.sum(sel[r1 * half:(r1 + 1) * half, :], axis=0, keepdims=True)
            if r1 == 0:
                c = c + jnp.sum(sel[half * half:half * half + half, :], axis=0, keepdims=True)
            counts.append(c)
        tail = sel[half * half + half:, :]
        for r1 in range(half, k):
            counts.append(tail[r1 - half:r1 - half + 1, :])
        nsel = jnp.zeros(s1.shape, F32)
        for r1 in range(k):
            nsel = jnp.where(rank1 == r1, counts[r1], nsel)
        rank2_ref[h] = rank2.astype(F32)
        nsel_ref[h] = nsel
        e1_ref[h] = jnp.exp(s1 - v1[0:1, :])
        e2z_ref[h] = jnp.exp(s2 - v2[0:1, :]) / z
        return 0

    lax.fori_loop(0, PEER_HEADS, head, 0)


def peer_select(s_t):
    rows, t = s_t.shape
    spec = pl.BlockSpec((PEER_HEADS, PEER_NKEYS, LANES), lambda i: (0, 0, i))
    shp = jax.ShapeDtypeStruct((PEER_HEADS, PEER_NKEYS, t), F32)
    return pl.pallas_call(
        _peer_select_kernel,
        grid=(t // LANES,),
        in_specs=[pl.BlockSpec((rows, LANES), lambda i: (0, i))],
        out_specs=[spec] * 4,
        out_shape=[shp] * 4,
        compiler_params=_params("parallel"),
        name="peer_select",
    )(s_t)


def _gelu(x):
    return 0.5 * x * (1.0 + lax.erf(x * (1.0 / math.sqrt(2.0))))


def _peer_dense_kernel(xn_ref, u_ref, vt_ref, rank2_ref, e2z_ref, nsel_ref, e1_ref, o_ref, a_ref):
    j = pl.program_id(1)
    nb = u_ref.shape[0] // PEER_NKEYS

    @pl.when(j == 0)
    def _():
        o_ref[...] = jnp.zeros_like(o_ref)

    for kb in range(nb):
        a = j * nb + kb
        rows = slice(kb * PEER_NKEYS, (kb + 1) * PEER_NKEYS)
        s = jnp.dot(u_ref[rows, :], xn_ref[...], preferred_element_type=F32)
        g = jnp.zeros(s.shape, F32)
        for h in range(PEER_HEADS):
            n_row = nsel_ref[h, pl.ds(a, 1), :]
            e_row = e1_ref[h, pl.ds(a, 1), :]
            g = g + jnp.where(rank2_ref[h] < n_row, e2z_ref[h], 0.0) * e_row
        a_ref[rows, :] = (_gelu(s) * g).astype(BF16)
    o_ref[...] += jnp.dot(vt_ref[...], a_ref[...], preferred_element_type=F32)


def peer_dense(xn_t, u_bf, vt_bf, rank2, e2z, nsel, e1, tm=512, te=512):
    d, t = xn_t.shape
    n_exp = u_bf.shape[0]
    sel_spec = pl.BlockSpec((PEER_HEADS, PEER_NKEYS, tm), lambda i, j: (0, 0, i))
    return pl.pallas_call(
        _peer_dense_kernel,
        grid=(t // tm, n_exp // te),
        in_specs=[pl.BlockSpec((d, tm), lambda i, j: (0, i)),
                  pl.BlockSpec((te, d), lambda i, j: (j, 0)),
                  pl.BlockSpec((d, te), lambda i, j: (0, j)),
                  sel_spec, sel_spec, sel_spec, sel_spec],
        out_specs=pl.BlockSpec((d, tm), lambda i, j: (0, i)),
        out_shape=jax.ShapeDtypeStruct((d, t), F32),
        scratch_shapes=[pltpu.VMEM((te, tm), BF16)],
        compiler_params=_params("parallel", "arbitrary"),
        name="peer_dense",
    )(xn_t, u_bf, vt_bf, rank2, e2z, nsel, e1)


def peer(x, g, w_q, sub_keys, u_tab, v_tab):
    t, d = x.shape
    xn = rmsnorm_bf16(x, g)
    xn_t = xn.T
    wq_t = w_q.T.astype(BF16)
    q_t = mm(wq_t, xn_t, _pick(wq_t.shape[0], (512, 256, 128)), _pick(t, (512, 256, 128)), BF16)
    hc = PEER_HEADS * 2
    dk2 = PEER_DK // 2
    keys = sub_keys.reshape(hc, PEER_NKEYS, dk2).astype(BF16)
    eye = jnp.eye(hc, dtype=BF16)
    kbd_t = (keys[:, :, None, :] * eye[:, None, :, None]).reshape(hc * PEER_NKEYS, hc * dk2)
    s_t = mm(kbd_t, q_t, _pick(kbd_t.shape[0], (512, 256, 128)), _pick(t, (512, 256, 128)))
    rank2, e2z, nsel, e1 = peer_select(s_t)
    out_t = peer_dense(xn_t, u_tab.astype(BF16), v_tab.T.astype(BF16), rank2, e2z, nsel, e1,
                       tm=_pick(t, (512, 256, 128)))
    return out_t.T


def _fox_norm_kernel(p_ref, gq_ref, gk_ref, q_ref, k_ref):
    scale = FOX_DH ** -0.5
    for h in range(FOX_HEADS):
        lo, hi = h * FOX_DH, (h + 1) * FOX_DH
        x = p_ref[:, lo:hi]
        y = x * lax.rsqrt(jnp.mean(x * x, axis=-1, keepdims=True) + RMS_EPS) * gq_ref[...]
        q_ref[:, lo:hi] = (y * scale).astype(q_ref.dtype)
        x = p_ref[:, FOX_W + lo:FOX_W + hi]
        k_ref[:, lo:hi] = x * lax.rsqrt(jnp.mean(x * x, axis=-1, keepdims=True) + RMS_EPS) * gk_ref[...]


def fox_norm(p, g_q, g_k):
    t = p.shape[0]
    tm = _pick(t, (256, 128))
    return pl.pallas_call(
        _fox_norm_kernel,
        grid=(t // tm,),
        in_specs=[pl.BlockSpec((tm, 2 * FOX_W), lambda i: (i, 0)),
                  pl.BlockSpec((1, FOX_DH), lambda i: (0, 0)), pl.BlockSpec((1, FOX_DH), lambda i: (0, 0))],
        out_specs=[pl.BlockSpec((tm, FOX_W), lambda i: (i, 0))] * 2,
        out_shape=[jax.ShapeDtypeStruct((t, FOX_W), BF16), jax.ShapeDtypeStruct((t, FOX_W), F32)],
        compiler_params=_params("parallel"),
        name="fox_norm",
    )(p, g_q.reshape(1, FOX_DH), g_k.reshape(1, FOX_DH))


def _fox_prompt_kernel(qi_ref, ki_ref, q_ref, k_ref, v_ref, f_ref, o_ref, m_ref, l_ref, acc_ref, *, tq, tk):
    n = pl.program_id(2)
    i = qi_ref[n]
    kk = ki_ref[n]

    @pl.when(kk == 0)
    def _():
        m_ref[...] = jnp.full_like(m_ref, NEG_INF)
        l_ref[...] = jnp.zeros_like(l_ref)
        acc_ref[...] = jnp.zeros_like(acc_ref)

    def step(masked):
        s = lax.dot_general(q_ref[...], k_ref[...].astype(BF16), (((1,), (1,)), ((), ())),
                            preferred_element_type=F32)
        s = s - f_ref[0]
        if masked:
            q_pos = i * tq + lax.broadcasted_iota(jnp.int32, s.shape, 0)
            k_pos = kk * tk + lax.broadcasted_iota(jnp.int32, s.shape, 1)
            s = jnp.where(q_pos >= k_pos, s, NEG_INF)
        m_prev = m_ref[...]
        m_new = jnp.maximum(m_prev, jnp.max(s, axis=1, keepdims=True))
        alpha = jnp.exp(m_prev - m_new)
        p = jnp.exp(s - m_new)
        l_ref[...] = alpha * l_ref[...] + jnp.sum(p, axis=1, keepdims=True)
        acc_ref[...] = alpha * acc_ref[...] + jnp.dot(p.astype(BF16), v_ref[...].astype(BF16),
                                                      preferred_element_type=F32)
        m_ref[...] = m_new

    on_diag = (kk + 1) * tk > i * tq + 1

    @pl.when(on_diag)
    def _():
        step(True)

    @pl.when(jnp.logical_not(on_diag))
    def _():
        step(False)

    @pl.when(kk == (i * tq + tq - 1) // tk)
    def _():
        o_ref[...] = (acc_ref[...] / l_ref[...]).astype(o_ref.dtype)


def fox_prompt(q_bf, k, p, f_cum, bsz, s_len, tq=1024, tk=1024):
    nq, nk = s_len // tq, s_len // tk
    pairs = [(i, kk) for i in range(nq) for kk in range((i * tq + tq - 1) // tk + 1)]
    qi = jnp.asarray([a for a, _ in pairs], jnp.int32)
    ki = jnp.asarray([b for _, b in pairs], jnp.int32)
    v_col0 = 2 * FOX_W // FOX_DH
    grid_spec = pltpu.PrefetchScalarGridSpec(
        num_scalar_prefetch=2,
        grid=(bsz, FOX_HEADS, len(pairs)),
        in_specs=[pl.BlockSpec((tq, FOX_DH), lambda b, h, n, qi, ki: (b * nq + qi[n], h)),
                  pl.BlockSpec((tk, FOX_DH), lambda b, h, n, qi, ki: (b * nk + ki[n], h)),
                  pl.BlockSpec((tk, FOX_DH), lambda b, h, n, qi, ki: (b * nk + ki[n], v_col0 + h)),
                  pl.BlockSpec((1, 1, tk), lambda b, h, n, qi, ki: (b * FOX_HEADS + h, 0, ki[n]))],
        out_specs=pl.BlockSpec((tq, FOX_DH), lambda b, h, n, qi, ki: (b * nq + qi[n], h)),
        scratch_shapes=[pltpu.VMEM((tq, 1), F32), pltpu.VMEM((tq, 1), F32), pltpu.VMEM((tq, FOX_DH), F32)],
    )
    return pl.pallas_call(
        functools.partial(_fox_prompt_kernel, tq=tq, tk=tk),
        grid_spec=grid_spec,
        out_shape=jax.ShapeDtypeStruct((bsz * s_len, FOX_W), BF16),
        compiler_params=_params("parallel", "parallel", "arbitrary"),
        name="fox_prompt",
    )(qi, ki, q_bf, k, p, f_cum)


SUB = 8


def _logf_cumsum_kernel(pt_ref, *refs):
    lf_refs, o_ref = refs[:-1], refs[-1]
    page, nh = lf_refs[0].shape[1:]
    tri = (lax.broadcasted_iota(jnp.int32, (page, page), 1)
           <= lax.broadcasted_iota(jnp.int32, (page, page), 0)).astype(F32)
    carry = jnp.zeros((1, nh), F32)
    for p, lf_ref in enumerate(lf_refs):
        f = jnp.dot(tri, lf_ref[0], precision=lax.Precision.HIGHEST, preferred_element_type=F32) + carry
        carry = f[page - 1:page, :]
        o_ref[0, p] = f


def paged_logf_cumsum(cache_logf, page_table):
    bsz, n_pages = page_table.shape
    _, page, nh = cache_logf.shape

    def page_spec(p):
        return pl.BlockSpec((1, page, nh), lambda b, pt: (pt[b, p], 0, 0))

    grid_spec = pltpu.PrefetchScalarGridSpec(
        num_scalar_prefetch=1,
        grid=(bsz,),
        in_specs=[page_spec(p) for p in range(n_pages)],
        out_specs=pl.BlockSpec((1, n_pages, page, nh), lambda b, pt: (b, 0, 0, 0)),
    )
    return pl.pallas_call(
        _logf_cumsum_kernel,
        grid_spec=grid_spec,
        out_shape=jax.ShapeDtypeStruct((bsz, n_pages, page, nh), F32),
        compiler_params=_params("parallel"),
        name="logf_cumsum",
    )(page_table, *([cache_logf] * n_pages))


def _fox_sample_kernel(pt_ref, q_ref, *refs, n_groups, group):
    kc_refs, vc_refs = refs[:group], refs[group:2 * group]
    f_ref, mask_ref, kn_ref, vn_ref, fn_ref, maskn_ref, o_ref, m_ref, l_ref, acc_ref = refs[2 * group:]
    pg = pl.program_id(1)

    @pl.when(pg == 0)
    def _():
        m_ref[...] = jnp.full_like(m_ref, NEG_INF)
        l_ref[...] = jnp.zeros_like(l_ref)
        acc_ref[...] = jnp.zeros_like(acc_ref)

    def attend(kvb):
        ss = [lax.dot_general(q_ref[0], k2, (((1,), (1,)), ((), ())), preferred_element_type=F32) + b
              for k2, _, b in kvb]
        m_prev = m_ref[...]
        m_new = m_prev
        for s in ss:
            m_new = jnp.maximum(m_new, jnp.max(s, axis=1, keepdims=True))
        alpha = jnp.exp(m_prev - m_new)
        l_new = alpha * l_ref[...]
        acc = alpha * acc_ref[...]
        for s, (_, v2, _) in zip(ss, kvb):
            p = jnp.exp(s - m_new)
            l_new = l_new + jnp.sum(p, axis=1, keepdims=True)
            acc = acc + jnp.dot(p.astype(BF16), v2, preferred_element_type=F32)
        l_ref[...] = l_new
        acc_ref[...] = acc
        m_ref[...] = m_new

    @pl.when(pg < n_groups)
    def _():
        rows = kc_refs[0].shape[0] * kc_refs[0].shape[1]
        attend([(kc[...].reshape(rows, FOX_DH).astype(BF16), vc[...].reshape(rows, FOX_DH).astype(BF16),
                 mask_ref[...] - f_ref[0, g])
                for g, (kc, vc) in enumerate(zip(kc_refs, vc_refs))])

    @pl.when(pg == n_groups)
    def _():
        attend([(kn_ref[0].astype(BF16), vn_ref[0].astype(BF16), maskn_ref[...] - fn_ref[0])])
        o_ref[0] = acc_ref[...] / l_ref[...]


def fox_sample(q_bf, k_new, v_new, lf_new, cache_k, cache_v, cache_logf, page_table):
    bsz, t_len, _ = q_bf.shape
    n_pool, page = cache_k.shape[:2]
    n_pages = page_table.shape[1]
    nh, dh = FOX_HEADS, FOX_DH
    rows = nh * t_len
    lanes = page * nh
    lanes_new = SUB * nh
    qf = q_bf.reshape(bsz, t_len, nh, dh).transpose(0, 2, 1, 3).reshape(bsz, rows, dh)
    f_pages = paged_logf_cumsum(cache_logf, page_table)
    f_rows = f_pages.reshape(bsz, n_pages, 1, lanes)
    pad = ((0, 0), (0, SUB - t_len), (0, 0))
    kn = jnp.pad(k_new, pad).reshape(bsz, lanes_new, dh)
    vn = jnp.pad(v_new, pad).reshape(bsz, lanes_new, dh)
    fn = (f_pages[:, -1, -1, :][:, None, :] + jnp.cumsum(jnp.pad(lf_new, pad), axis=1)).reshape(bsz, 1, lanes_new)
    row_head, row_q = np.arange(rows) // t_len, np.arange(rows) % t_len
    mask = np.where(row_head[:, None] == (np.arange(lanes) % nh)[None, :], 0.0, NEG_INF).astype(np.float32)
    key_n, head_n = np.arange(lanes_new) // nh, np.arange(lanes_new) % nh
    ok = (row_head[:, None] == head_n[None, :]) & (key_n[None, :] <= row_q[:, None]) & (key_n[None, :] < t_len)
    mask_new = np.where(ok, 0.0, NEG_INF).astype(np.float32)
    group = _pick(n_pages, (8, 4, 2, 1))
    n_groups = n_pages // group
    last = n_groups - 1

    def cache_spec(g):
        return pl.BlockSpec((page, nh, dh), lambda b, p, pt: (pt[b, jnp.minimum(p, last) * group + g], 0, 0))

    cache_specs = [cache_spec(g) for g in range(group)]
    kc3 = cache_k.reshape(n_pool * page, nh, dh)
    vc3 = cache_v.reshape(n_pool * page, nh, dh)
    grid_spec = pltpu.PrefetchScalarGridSpec(
        num_scalar_prefetch=1,
        grid=(bsz, n_groups + 1),
        in_specs=[pl.BlockSpec((1, rows, dh), lambda b, p, pt: (b, 0, 0)),
                  *cache_specs, *cache_specs,
                  pl.BlockSpec((1, group, 1, lanes), lambda b, p, pt: (b, jnp.minimum(p, last), 0, 0)),
                  pl.BlockSpec((rows, lanes), lambda b, p, pt: (0, 0)),
                  pl.BlockSpec((1, lanes_new, dh), lambda b, p, pt: (b, 0, 0)),
                  pl.BlockSpec((1, lanes_new, dh), lambda b, p, pt: (b, 0, 0)),
                  pl.BlockSpec((1, 1, lanes_new), lambda b, p, pt: (b, 0, 0)),
                  pl.BlockSpec((rows, lanes_new), lambda b, p, pt: (0, 0))],
        out_specs=pl.BlockSpec((1, rows, dh), lambda b, p, pt: (b, 0, 0)),
        scratch_shapes=[pltpu.VMEM((rows, 1), F32), pltpu.VMEM((rows, 1), F32), pltpu.VMEM((rows, dh), F32)],
    )
    out = pl.pallas_call(
        functools.partial(_fox_sample_kernel, n_groups=n_groups, group=group),
        grid_spec=grid_spec,
        out_shape=jax.ShapeDtypeStruct((bsz, rows, dh), F32),
        compiler_params=_params("parallel", "arbitrary"),
        name="fox_sample",
    )(page_table, qf, *([kc3] * group), *([vc3] * group),
      f_rows, jnp.asarray(mask), kn, vn, fn, jnp.asarray(mask_new))
    return out.reshape(bsz, nh, t_len, dh).transpose(0, 2, 1, 3).reshape(bsz, t_len, nh * dh)


def _mlstm_kernel(q_ref, k_ref, v_ref, o_ref, g_ref, c0_ref, n0_ref, m0_ref, mg_ref,
                  h_ref, c1_ref, n1_ref, m1_ref, c_scr, n_scr, m_scr):
    c = pl.program_id(1)
    L = q_ref.shape[0]
    nh, dh = MLSTM_HEADS, MLSTM_DH

    @pl.when(c == 0)
    def _():
        c_scr[...] = c0_ref[0]
        n_scr[...] = n0_ref[0]
        m_scr[...] = m0_ref[0]

    t_idx = lax.broadcasted_iota(jnp.int32, (L, L), 0)
    s_idx = lax.broadcasted_iota(jnp.int32, (L, L), 1)
    causal = s_idx <= t_idx
    diag = s_idx == t_idx
    for h in range(nh):
        cols = slice(h * dh, (h + 1) * dh)
        ig = g_ref[0, h:h + 1, :]
        lf = g_ref[0, nh + h:nh + h + 1, :]
        m0 = m_scr[h:h + 1, :]
        f_col = jnp.sum(jnp.where(causal, lf, 0.0), axis=1, keepdims=True)
        ig_col = jnp.sum(jnp.where(diag, ig, 0.0), axis=1, keepdims=True)
        g_col = ig_col - f_col
        g_row = jnp.sum(jnp.where(diag, g_col, 0.0), axis=0, keepdims=True)
        cm_col = jnp.max(jnp.where(causal, g_row, NEG_INF), axis=1, keepdims=True)
        m_col = f_col + jnp.maximum(m0, cm_col)
        b_col = f_col - m_col
        a_col = jnp.exp(f_col + m0 - m_col)
        d = jnp.exp(jnp.where(causal, b_col + g_row, NEG_INF))
        q = q_ref[:, cols] * (dh ** -0.5)
        k = k_ref[:, cols]
        v = v_ref[:, cols]
        qb, kb, vb = q.astype(BF16), k.astype(BF16), v.astype(BF16)
        qk = lax.dot_general(qb, kb, (((1,), (1,)), ((), ())), preferred_element_type=F32) * d
        c_old = c_scr[h]
        n_old = n_scr[h:h + 1, :]
        num = a_col * jnp.dot(qb, c_old.astype(BF16), preferred_element_type=F32) \
            + jnp.dot(qk.astype(BF16), vb, preferred_element_type=F32)
        den = a_col * jnp.sum(q * n_old, axis=1, keepdims=True) + jnp.sum(qk, axis=1, keepdims=True)
        hs = num / jnp.maximum(jnp.abs(den), jnp.exp(-m_col))
        hn = hs * lax.rsqrt(jnp.mean(hs * hs, axis=1, keepdims=True) + RMS_EPS) * mg_ref[:, cols]
        h_ref[:, cols] = (hn * jax.nn.sigmoid(o_ref[:, cols])).astype(h_ref.dtype)
        b_last = b_col[L - 1:L, :]
        a_last = a_col[L - 1:L, :]
        kd = k * jnp.exp(b_last + g_col)
        c_scr[h] = a_last * c_old + lax.dot_general(kd.astype(BF16), vb, (((0,), (0,)), ((), ())),
                                                     preferred_element_type=F32)
        n_scr[h:h + 1, :] = a_last * n_old + jnp.sum(kd, axis=0, keepdims=True)
        m_scr[h:h + 1, :] = m_col[L - 1:L, :]

    @pl.when(c == pl.num_programs(1) - 1)
    def _():
        c1_ref[0] = c_scr[...]
        n1_ref[0] = n_scr[...]
        m1_ref[0] = m_scr[...]


def mlstm(p, gates, c0, n0, m0, mh_g, bsz, n_chunks, chunk, col0):
    nh, dh, mw = MLSTM_HEADS, MLSTM_DH, MLSTM_W
    rows = bsz * n_chunks * chunk

    def col_spec(j):
        return pl.BlockSpec((chunk, mw), lambda b, c: (b * n_chunks + c, col0 + j))

    state = lambda b, c: (b, 0, 0)
    return pl.pallas_call(
        _mlstm_kernel,
        grid=(bsz, n_chunks),
        in_specs=[col_spec(0), col_spec(1), col_spec(2), col_spec(3),
                  pl.BlockSpec((1, 2 * nh, chunk), lambda b, c: (b * n_chunks + c, 0, 0)),
                  pl.BlockSpec((1, nh, dh, dh), lambda b, c: (b, 0, 0, 0)),
                  pl.BlockSpec((1, nh, dh), state), pl.BlockSpec((1, nh, 1), state),
                  pl.BlockSpec((1, mw), lambda b, c: (0, 0))],
        out_specs=[pl.BlockSpec((chunk, mw), lambda b, c: (b * n_chunks + c, 0)),
                   pl.BlockSpec((1, nh, dh, dh), lambda b, c: (b, 0, 0, 0)),
                   pl.BlockSpec((1, nh, dh), state), pl.BlockSpec((1, nh, 1), state)],
        out_shape=[jax.ShapeDtypeStruct((rows, mw), BF16),
                   jax.ShapeDtypeStruct((bsz, nh, dh, dh), F32),
                   jax.ShapeDtypeStruct((bsz, nh, dh), F32),
                   jax.ShapeDtypeStruct((bsz, nh, 1), F32)],
        scratch_shapes=[pltpu.VMEM((nh, dh, dh), F32), pltpu.VMEM((nh, dh), F32), pltpu.VMEM((nh, 1), F32)],
        compiler_params=_params("parallel", "arbitrary"),
        name="mlstm",
    )(p, p, p, p, gates, c0, n0, m0.reshape(bsz, nh, 1), mh_g.reshape(1, mw))


CONV_HALO = 32
CONV_ROWS = 32


def _group_ln_swish(y, g, b):
    outs = []
    for grp in range(CONV_GROUPS):
        x = y[:, grp * LANES:(grp + 1) * LANES]
        mu = jnp.mean(x, axis=1, keepdims=True)
        xc = x - mu
        var = jnp.mean(xc * xc, axis=1, keepdims=True)
        z = xc * lax.rsqrt(var + LN_EPS) * g[:, grp * LANES:(grp + 1) * LANES] + b[:, grp * LANES:(grp + 1) * LANES]
        outs.append(z * jax.nn.sigmoid(z))
    return jnp.concatenate(outs, axis=1)


def _conv_prompt_kernel(cv_ref, cg_ref, hv_ref, hg_ref, w_ref, b_ref, g_ref, beta_ref, y_ref, buf_ref, ext_ref):
    i = pl.program_id(1)
    ts = cv_ref.shape[0]
    hist = jnp.where(i > 0, hv_ref[...] * jax.nn.sigmoid(hg_ref[...]), 0.0)
    ext_ref[0:CONV_HALO, :] = hist
    ext_ref[CONV_HALO:, :] = cv_ref[...] * jax.nn.sigmoid(cg_ref[...])
    off = CONV_HALO - (CONV_K - 1)
    for r0 in range(0, ts, CONV_ROWS):
        acc = jnp.zeros((CONV_ROWS, CONV_W), F32) + b_ref[...]
        for j in range(CONV_K):
            acc = acc + ext_ref[r0 + off + j:r0 + off + j + CONV_ROWS, :] * w_ref[j:j + 1, :]
        y_ref[r0:r0 + CONV_ROWS, :] = _group_ln_swish(acc, g_ref[...], beta_ref[...]).astype(y_ref.dtype)

    @pl.when(i == pl.num_programs(1) - 1)
    def _():
        buf_ref[0] = ext_ref[ts + off:ts + CONV_HALO, :]


def conv_prompt(p, conv_w, conv_b, cn_g, cn_b, bsz, s_len, ts=512):
    nt = s_len // ts
    hb = ts // CONV_HALO

    def halo(col):
        return pl.BlockSpec((CONV_HALO, CONV_W), lambda b, i: (jnp.maximum((b * nt + i) * hb - 1, 0), col))

    vec = pl.BlockSpec((1, CONV_W), lambda b, i: (0, 0))
    return pl.pallas_call(
        _conv_prompt_kernel,
        grid=(bsz, nt),
        in_specs=[pl.BlockSpec((ts, CONV_W), lambda b, i: (b * nt + i, 0)),
                  pl.BlockSpec((ts, CONV_W), lambda b, i: (b * nt + i, 1)),
                  halo(0), halo(1),
                  pl.BlockSpec((CONV_K, CONV_W), lambda b, i: (0, 0)), vec, vec, vec],
        out_specs=[pl.BlockSpec((ts, CONV_W), lambda b, i: (b * nt + i, 0)),
                   pl.BlockSpec((1, CONV_K - 1, CONV_W), lambda b, i: (b, 0, 0))],
        out_shape=[jax.ShapeDtypeStruct((bsz * s_len, CONV_W), BF16),
                   jax.ShapeDtypeStruct((bsz, CONV_K - 1, CONV_W), F32)],
        scratch_shapes=[pltpu.VMEM((ts + CONV_HALO, CONV_W), F32)],
        compiler_params=_params("parallel", "arbitrary"),
        name="conv_prompt",
    )(p, p, p, p, conv_w, conv_b.reshape(1, -1), cn_g.reshape(1, -1), cn_b.reshape(1, -1))


def _conv_sample_kernel(cv_ref, cg_ref, st_ref, w_ref, b_ref, g_ref, beta_ref, y_ref, buf_ref, ext_ref, *, t_len):
    nb = st_ref.shape[0]
    hist = CONV_K - 1
    y_ref[...] = jnp.zeros_like(y_ref)
    for bb in range(nb):
        rows = slice(bb * SUB, bb * SUB + t_len)
        ext_ref[0:hist, :] = st_ref[bb]
        ext_ref[hist:hist + t_len, :] = cv_ref[rows, :] * jax.nn.sigmoid(cg_ref[rows, :])
        ys = [jnp.sum(ext_ref[t:t + CONV_K, :] * w_ref[...], axis=0, keepdims=True) for t in range(t_len)]
        y = jnp.concatenate(ys, axis=0) + b_ref[...]
        y_ref[rows, :] = _group_ln_swish(y, g_ref[...], beta_ref[...]).astype(y_ref.dtype)
        buf_ref[bb] = ext_ref[t_len:t_len + hist, :]


def conv_sample(p_pad, state_conv, conv_w, conv_b, cn_g, cn_b, t_len, nb=8):
    bsz = state_conv.shape[0]
    vec = pl.BlockSpec((1, CONV_W), lambda i: (0, 0))
    return pl.pallas_call(
        functools.partial(_conv_sample_kernel, t_len=t_len),
        grid=(bsz // nb,),
        in_specs=[pl.BlockSpec((nb * SUB, CONV_W), lambda i: (i, 0)),
                  pl.BlockSpec((nb * SUB, CONV_W), lambda i: (i, 1)),
                  pl.BlockSpec((nb, CONV_K - 1, CONV_W), lambda i: (i, 0, 0)),
                  pl.BlockSpec((CONV_K, CONV_W), lambda i: (0, 0)), vec, vec, vec],
        out_specs=[pl.BlockSpec((nb * SUB, CONV_W), lambda i: (i, 0)),
                   pl.BlockSpec((nb, CONV_K - 1, CONV_W), lambda i: (i, 0, 0))],
        out_shape=[jax.ShapeDtypeStruct((bsz * SUB, CONV_W), BF16),
                   jax.ShapeDtypeStruct((bsz, CONV_K - 1, CONV_W), F32)],
        scratch_shapes=[pltpu.VMEM((CONV_K - 1 + SUB, CONV_W), F32)],
        compiler_params=_params("parallel"),
        name="conv_sample",
    )(p_pad, p_pad, state_conv, conv_w, conv_b.reshape(1, -1), cn_g.reshape(1, -1), cn_b.reshape(1, -1))


def kernel(x_prompt, x_sample, state_conv, state_C, state_n, state_m, cache_k, cache_v, cache_logf, page_table,
           norm0_mix, w_in0, b_ig0, b_fg0, conv_w0, conv_b0, conv_norm_g0, conv_norm_b0, mlstm_norm_g0, w_out0,
           norm1_mix, w_in1, b_f1, q_norm_g1, k_norm_g1, w_out1,
           norm0_ffn, peer_wq0, peer_keys0, peer_u0, peer_v0,
           norm1_ffn, peer_wq1, peer_keys1, peer_u1, peer_v1):
    bp, sp, d = x_prompt.shape
    bs, ss, _ = x_sample.shape
    tp, ts = bp * sp, bs * ss
    x = jnp.concatenate([x_prompt.reshape(tp, d), x_sample.reshape(ts, d)], axis=0)

    n_main0 = 2 * CONV_W + 4 * MLSTM_W
    xn = rmsnorm_bf16(x, norm0_mix)
    p0 = mm_tokens(xn, w_in0[:, :n_main0].astype(BF16))
    g0 = gate_proj(x, norm0_mix, w_in0[:, n_main0:])
    nh = MLSTM_HEADS
    gates = jnp.concatenate([g0[:, :nh] + b_ig0, jax.nn.log_sigmoid(g0[:, nh:] + b_fg0)], axis=1)
    conv_par = (conv_w0, conv_b0, conv_norm_g0, conv_norm_b0)
    mcol0 = 2 * CONV_W // MLSTM_W
    nc = sp // MLSTM_CHUNK
    gates_p = gates[:tp].reshape(bp * nc, MLSTM_CHUNK, 2 * nh).transpose(0, 2, 1)
    hp, p_C, p_n, p_m = mlstm(p0, gates_p, jnp.zeros((bp, nh, MLSTM_DH, MLSTM_DH), F32),
                              jnp.zeros((bp, nh, MLSTM_DH), F32), jnp.zeros((bp, nh), F32),
                              mlstm_norm_g0, bp, nc, MLSTM_CHUNK, mcol0)
    yp, p_conv = conv_prompt(p0, *conv_par, bp, sp)
    tpad = ((0, 0), (0, SUB - ss), (0, 0))
    p0s = jnp.pad(p0[tp:].reshape(bs, ss, -1), tpad).reshape(bs * SUB, -1)
    gs = gates[tp:].reshape(bs, ss, 2 * nh)
    gates_s = jnp.concatenate([jnp.pad(gs[..., :nh], tpad, constant_values=NEG_INF),
                               jnp.pad(gs[..., nh:], tpad)], axis=-1).transpose(0, 2, 1)
    hs, s_C, s_n, s_m = mlstm(p0s, gates_s, state_C, state_n, state_m, mlstm_norm_g0, bs, 1, SUB, mcol0)
    ys, s_conv = conv_sample(p0s, state_conv, *conv_par, ss)
    unpad = lambda a: a.reshape(bs, SUB, -1)[:, :ss].reshape(ts, -1)
    y_all = jnp.concatenate([yp, unpad(ys)], axis=0)
    h_all = jnp.concatenate([hp[:tp], unpad(hs)], axis=0)
    w_out0_bf = w_out0.astype(BF16)
    x = x + mm2(y_all, h_all, w_out0_bf[:CONV_W], w_out0_bf[CONV_W:], _pick(tp + ts, (512, 256, 128)), 1024)
    p_m, s_m = p_m.reshape(bp, nh), s_m.reshape(bs, nh)
    x = x + peer(x, norm0_ffn, peer_wq0, peer_keys0, peer_u0, peer_v0)

    xn = rmsnorm_bf16(x, norm1_mix)
    p1 = mm_tokens(xn, w_in1[:, :3 * FOX_W].astype(BF16))
    g1 = gate_proj(x, norm1_mix, w_in1[:, 3 * FOX_W:])
    lf = jax.nn.log_sigmoid(g1 + b_f1)
    q_bf, k = fox_norm(p1, q_norm_g1, k_norm_g1)
    v = p1[:, 2 * FOX_W:]
    f_cum = jnp.cumsum(lf[:tp].reshape(bp, sp, FOX_HEADS), axis=1).transpose(0, 2, 1).reshape(bp * FOX_HEADS, 1, sp)
    attn_p = fox_prompt(q_bf, k, p1, f_cum, bp, sp)
    attn_s = fox_sample(q_bf[tp:].reshape(bs, ss, FOX_W), k[tp:].reshape(bs, ss, FOX_W),
                        v[tp:].reshape(bs, ss, FOX_W), lf[tp:].reshape(bs, ss, FOX_HEADS),
                        cache_k, cache_v, cache_logf, page_table)
    attn = jnp.concatenate([attn_p, attn_s.reshape(ts, FOX_W).astype(BF16)], axis=0)
    x = x + mm_tokens(attn, w_out1.astype(BF16))
    x = x + peer(x, norm1_ffn, peer_wq1, peer_keys1, peer_u1, peer_v1)

    hd = (FOX_HEADS, FOX_DH)
    return (x[:tp].reshape(bp, sp, d), x[tp:].reshape(bs, ss, d), p_conv, p_C, p_n, p_m,
            k[:tp].reshape(bp, sp, *hd), v[:tp].reshape(bp, sp, *hd), lf[:tp].reshape(bp, sp, FOX_HEADS),
            s_conv, s_C, s_n, s_m,
            k[tp:].reshape(bs, ss, *hd), v[tp:].reshape(bs, ss, *hd), lf[tp:].reshape(bs, ss, FOX_HEADS))
```

```python
import functools
import math

import jax
import jax.numpy as jnp
import numpy as np
from jax import lax
from jax.experimental import pallas as pl
from jax.experimental.pallas import tpu as pltpu

F32 = jnp.float32
BF16 = jnp.bfloat16

D_MODEL = 2048
CONV_W = 1024
CONV_K = 31
CONV_GROUPS = 8
MLSTM_HEADS = 4
MLSTM_W = 1024
MLSTM_DH = 256
MLSTM_CHUNK = 128
FOX_HEADS = 16
FOX_DH = 128
FOX_W = 2048
PEER_HEADS = 8
PEER_NKEYS = 128
PEER_DK = 128
PEER_TOPK = 16
RMS_EPS = 1e-6
LN_EPS = 1e-5

LANES = 128
VMEM_LIMIT = 56 * 1024 * 1024
NEG_INF = float("-inf")


def _params(*sem):
    return pltpu.CompilerParams(dimension_semantics=sem, vmem_limit_bytes=VMEM_LIMIT)


def _rmsnorm_kernel(x_ref, g_ref, o_ref):
    x = x_ref[...]
    y = x * lax.rsqrt(jnp.mean(x * x, axis=-1, keepdims=True) + RMS_EPS)
    o_ref[...] = (y * g_ref[...]).astype(o_ref.dtype)


def rmsnorm_bf16(x, g):
    t, d = x.shape
    tm = _pick(t, (512, 256, 128))
    return pl.pallas_call(
        _rmsnorm_kernel,
        grid=(t // tm,),
        in_specs=[pl.BlockSpec((tm, d), lambda i: (i, 0)), pl.BlockSpec((1, d), lambda i: (0, 0))],
        out_specs=pl.BlockSpec((tm, d), lambda i: (i, 0)),
        out_shape=jax.ShapeDtypeStruct((t, d), BF16),
        compiler_params=_params("parallel"),
        name="rmsnorm",
    )(x, g.reshape(1, d))


def _rmsnorm_t_kernel(x_ref, g_ref, o_ref):
    x = x_ref[...]
    y = x * lax.rsqrt(jnp.mean(x * x, axis=-1, keepdims=True) + RMS_EPS) * g_ref[...]
    o_ref[...] = y.T.astype(o_ref.dtype)


def rmsnorm_bf16_t(x, g):
    t, d = x.shape
    tm = _pick(t, (512, 256, 128))
    return pl.pallas_call(
        _rmsnorm_t_kernel,
        grid=(t // tm,),
        in_specs=[pl.BlockSpec((tm, d), lambda i: (i, 0)), pl.BlockSpec((1, d), lambda i: (0, 0))],
        out_specs=pl.BlockSpec((d, tm), lambda i: (0, i)),
        out_shape=jax.ShapeDtypeStruct((d, t), BF16),
        compiler_params=_params("parallel"),
        name="rmsnorm_t",
    )(x, g.reshape(1, d))


def _add_t_kernel(x_ref, yt_ref, o_ref):
    o_ref[...] = x_ref[...] + yt_ref[...].T


def add_transposed(x, y_t):
    t, d = x.shape
    tm = _pick(t, (512, 256, 128))
    return pl.pallas_call(
        _add_t_kernel,
        grid=(t // tm,),
        in_specs=[pl.BlockSpec((tm, d), lambda i: (i, 0)), pl.BlockSpec((d, tm), lambda i: (0, i))],
        out_specs=pl.BlockSpec((tm, d), lambda i: (i, 0)),
        out_shape=jax.ShapeDtypeStruct((t, d), F32),
        compiler_params=_params("parallel"),
        name="add_t",
    )(x, y_t)


def _gate_proj_kernel(x_ref, g_ref, w_ref, o_ref):
    x = x_ref[...]
    y = x * lax.rsqrt(jnp.mean(x * x, axis=-1, keepdims=True) + RMS_EPS) * g_ref[...]
    o_ref[...] = jnp.dot(y, w_ref[...], precision=lax.Precision.HIGHEST, preferred_element_type=F32)


def gate_proj(x, g, w):
    t, d = x.shape
    n = w.shape[1]
    tm = _pick(t, (256, 128))
    out = pl.pallas_call(
        _gate_proj_kernel,
        grid=(t // tm,),
        in_specs=[pl.BlockSpec((tm, d), lambda i: (i, 0)), pl.BlockSpec((1, d), lambda i: (0, 0)),
                  pl.BlockSpec((d, LANES), lambda i: (0, 0))],
        out_specs=pl.BlockSpec((tm, LANES), lambda i: (i, 0)),
        out_shape=jax.ShapeDtypeStruct((t, LANES), F32),
        compiler_params=_params("parallel"),
        name="gate_proj",
    )(x, g.reshape(1, d), jnp.pad(w, ((0, 0), (0, LANES - n))))
    return out[:, :n]


def _mm_kernel(x_ref, w_ref, o_ref):
    o_ref[...] = jnp.dot(x_ref[...], w_ref[...], preferred_element_type=F32).astype(o_ref.dtype)


def mm(x, w, tm, tn, out_dtype=F32):
    m, k = x.shape
    n = w.shape[1]
    assert m % tm == 0 and n % tn == 0, (m, n, tm, tn)
    return pl.pallas_call(
        _mm_kernel,
        grid=(n // tn, m // tm),
        in_specs=[pl.BlockSpec((tm, k), lambda j, i: (i, 0)), pl.BlockSpec((k, tn), lambda j, i: (0, j))],
        out_specs=pl.BlockSpec((tm, tn), lambda j, i: (i, j)),
        out_shape=jax.ShapeDtypeStruct((m, n), out_dtype),
        compiler_params=_params("parallel", "parallel"),
        name="mm",
    )(x, w)


def _mm2_kernel(x1_ref, x2_ref, w1_ref, w2_ref, o_ref):
    o_ref[...] = (jnp.dot(x1_ref[...], w1_ref[...], preferred_element_type=F32)
                  + jnp.dot(x2_ref[...], w2_ref[...], preferred_element_type=F32))


def mm2(x1, x2, w1, w2, tm, tn):
    m, k1 = x1.shape
    k2 = x2.shape[1]
    n = w1.shape[1]
    return pl.pallas_call(
        _mm2_kernel,
        grid=(n // tn, m // tm),
        in_specs=[pl.BlockSpec((tm, k1), lambda j, i: (i, 0)), pl.BlockSpec((tm, k2), lambda j, i: (i, 0)),
                  pl.BlockSpec((k1, tn), lambda j, i: (0, j)), pl.BlockSpec((k2, tn), lambda j, i: (0, j))],
        out_specs=pl.BlockSpec((tm, tn), lambda j, i: (i, j)),
        out_shape=jax.ShapeDtypeStruct((m, n), F32),
        compiler_params=_params("parallel", "parallel"),
        name="mm2",
    )(x1, x2, w1, w2)


def _pick(n, cands):
    for c in cands:
        if n % c == 0:
            return c
    raise ValueError(n)


def mm_tokens(x, w, out_dtype=F32):
    t = x.shape[0]
    n = w.shape[1]
    return mm(x, w, _pick(t, (512, 256, 128)), _pick(n, (1024, 512, 256, 128)), out_dtype)


def _topk_rows(arrays, k):
    n, lanes = arrays[0].shape
    iota = lax.broadcasted_iota(jnp.int32, (n, lanes), 0)
    iota_k = lax.broadcasted_iota(jnp.int32, (k, lanes), 0)
    unranked = jnp.full((n, lanes), k, jnp.int32)
    zeros_k = jnp.zeros((k, lanes), F32)

    def run(exact_ties):
        carries_rank = [exact_ties or i == len(arrays) - 1 for i in range(len(arrays))]

        def body(r, carry):
            out = []
            for (w, rank, vals), has_rank in zip(carry, carries_rank):
                mx = jnp.max(w, axis=0, keepdims=True)
                hit = w == mx
                if exact_ties:
                    hit = iota == jnp.min(jnp.where(hit, iota, n), axis=0, keepdims=True)
                out.append((jnp.where(hit, NEG_INF, w), jnp.where(hit, r, rank) if has_rank else rank,
                            jnp.where(iota_k == r, mx, vals)))
            return tuple(out)

        init = tuple((s, unranked if has_rank else 0, zeros_k) for s, has_rank in zip(arrays, carries_rank))
        res = []
        for s, (_, rank, vals), has_rank in zip(arrays, lax.fori_loop(0, k, body, init), carries_rank):
            if not has_rank:
                rank = unranked
                for r in range(k):
                    rank = jnp.where(s == vals[r:r + 1, :], r, rank)
            res.append((rank, vals))
        return tuple(res)

    fast = run(False)
    ranked = sum(jnp.sum((rank < k).astype(jnp.int32), axis=0, keepdims=True) for rank, _ in fast)
    clean = jnp.max(ranked) == k * len(arrays)
    res = lax.cond(clean, lambda: fast, lambda: run(True))
    return [(vals, rank) for rank, vals in res]


def _peer_select_kernel(s_ref, rank2_ref, e2z_ref, nsel_ref, e1_ref):
    k = PEER_TOPK
    half = k // 2
    L = s_ref.shape[1]
    sub = lax.broadcasted_iota(jnp.int32, (half, L), 0)

    def head(h, _):
        base = pl.multiple_of(h * (2 * PEER_NKEYS), 2 * PEER_NKEYS)
        s1 = s_ref[pl.ds(base, PEER_NKEYS), :]
        s2 = s_ref[pl.ds(base + PEER_NKEYS, PEER_NKEYS), :]
        (v1, rank1), (v2, rank2) = _topk_rows([s1, s2], k)
        pieces, pos = [], []
        for r1 in range(half):
            pieces.append(v1[r1:r1 + 1, :] + v2[0:half, :])
            pos.append(sub + r1 * k)
        pieces.append(v1[0:1, :] + v2[half:k, :])
        pos.append(sub + half)
        pieces.append(v1[half:k, :] + v2[0:1, :])
        pos.append((sub + half) * k)
        cand = jnp.concatenate(pieces, axis=0)
        posid = jnp.concatenate(pos, axis=0)
        big = k * k

        def pick_all(exact_ties):
            def pick(_, w):
                mx = jnp.max(w, axis=0, keepdims=True)
                hit = w == mx
                if exact_ties:
                    hit = posid == jnp.min(jnp.where(hit, posid, big), axis=0, keepdims=True)
                return jnp.where(hit, NEG_INF, w)

            return (lax.fori_loop(0, k, pick, cand) == NEG_INF).astype(F32)

        sel_fast = pick_all(False)
        picked = jnp.sum(sel_fast, axis=0, keepdims=True)
        sel = lax.cond(jnp.max(picked) == k, lambda: sel_fast, lambda: pick_all(True))
        top = v1[0:1, :] + v2[0:1, :]
        z = jnp.sum(sel * jnp.exp(cand - top), axis=0, keepdims=True)
        counts = []
        for r1 in range(half):
            c = jnp.sum(sel[r1 * half:(r1 + 1) * half, :], axis=0, keepdims=True)
            if r1 == 0:
                c = c + jnp.sum(sel[half * half:half * half + half, :], axis=0, keepdims=True)
            counts.append(c)
        tail = sel[half * half + half:, :]
        for r1 in range(half, k):
            counts.append(tail[r1 - half:r1 - half + 1, :])
        nsel = jnp.zeros(s1.shape, F32)
        for r1 in range(k):
            nsel = jnp.where(rank1 == r1, counts[r1], nsel)
        rank2_ref[h] = rank2.astype(F32)
        nsel_ref[h] = nsel
        e1_ref[h] = jnp.exp(s1 - v1[0:1, :])
        e2z_ref[h] = jnp.exp(s2 - v2[0:1, :]) / z
        return 0

    lax.fori_loop(0, PEER_HEADS, head, 0)


def peer_select(s_t):
    rows, t = s_t.shape
    spec = pl.BlockSpec((PEER_HEADS, PEER_NKEYS, LANES), lambda i: (0, 0, i))
    shp = jax.ShapeDtypeStruct((PEER_HEADS, PEER_NKEYS, t), F32)
    return pl.pallas_call(
        _peer_select_kernel,
        grid=(t // LANES,),
        in_specs=[pl.BlockSpec((rows, LANES), lambda i: (0, i))],
        out_specs=[spec] * 4,
        out_shape=[shp] * 4,
        compiler_params=_params("parallel"),
        name="peer_select",
    )(s_t)


def _gelu(x):
    return 0.5 * x * (1.0 + lax.erf(x * (1.0 / math.sqrt(2.0))))


def _peer_dense_kernel(xn_ref, u_ref, vt_ref, rank2_ref, e2z_ref, nsel_ref, e1_ref, o_ref, a_ref):
    j = pl.program_id(1)
    nb = u_ref.shape[0] // PEER_NKEYS

    @pl.when(j == 0)
    def _():
        o_ref[...] = jnp.zeros_like(o_ref)

    for kb in range(nb):
        a = j * nb + kb
        rows = slice(kb * PEER_NKEYS, (kb + 1) * PEER_NKEYS)
        s = jnp.dot(u_ref[rows, :], xn_ref[...], preferred_element_type=F32)
        g = jnp.zeros(s.shape, F32)
        for h in range(PEER_HEADS):
            n_row = nsel_ref[h, pl.ds(a, 1), :]
            e_row = e1_ref[h, pl.ds(a, 1), :]
            g = g + jnp.where(rank2_ref[h] < n_row, e2z_ref[h], 0.0) * e_row
        a_ref[rows, :] = (_gelu(s) * g).astype(BF16)
    o_ref[...] += jnp.dot(vt_ref[...], a_ref[...], preferred_element_type=F32)


def peer_dense(xn_t, u_bf, vt_bf, rank2, e2z, nsel, e1, tm=512, te=512):
    d, t = xn_t.shape
    n_exp = u_bf.shape[0]
    sel_spec = pl.BlockSpec((PEER_HEADS, PEER_NKEYS, tm), lambda i, j: (0, 0, i))
    return pl.pallas_call(
        _peer_dense_kernel,
        grid=(t // tm, n_exp // te),
        in_specs=[pl.BlockSpec((d, tm), lambda i, j: (0, i)),
                  pl.BlockSpec((te, d), lambda i, j: (j, 0)),
                  pl.BlockSpec((d, te), lambda i, j: (0, j)),
                  sel_spec, sel_spec, sel_spec, sel_spec],
        out_specs=pl.BlockSpec((d, tm), lambda i, j: (0, i)),
        out_shape=jax.ShapeDtypeStruct((d, t), F32),
        scratch_shapes=[pltpu.VMEM((te, tm), BF16)],
        compiler_params=_params("parallel", "arbitrary"),
        name="peer_dense",
    )(xn_t, u_bf, vt_bf, rank2, e2z, nsel, e1)


def peer(x, g, w_q, sub_keys, u_tab, v_tab):
    t, d = x.shape
    xn_t = rmsnorm_bf16_t(x, g)
    wq_t = w_q.T.astype(BF16)
    q_t = mm(wq_t, xn_t, _pick(wq_t.shape[0], (512, 256, 128)), _pick(t, (512, 256, 128)), BF16)
    hc = PEER_HEADS * 2
    dk2 = PEER_DK // 2
    keys = sub_keys.reshape(hc, PEER_NKEYS, dk2).astype(BF16)
    eye = jnp.eye(hc, dtype=BF16)
    kbd_t = (keys[:, :, None, :] * eye[:, None, :, None]).reshape(hc * PEER_NKEYS, hc * dk2)
    s_t = mm(kbd_t, q_t, _pick(kbd_t.shape[0], (512, 256, 128)), _pick(t, (512, 256, 128)))
    rank2, e2z, nsel, e1 = peer_select(s_t)
    out_t = peer_dense(xn_t, u_tab.astype(BF16), v_tab.T.astype(BF16), rank2, e2z, nsel, e1,
                       tm=_pick(t, (512, 256, 128)))
    return add_transposed(x, out_t)


def _fox_norm_kernel(p_ref, gq_ref, gk_ref, q_ref, k_ref):
    scale = FOX_DH ** -0.5
    for h in range(FOX_HEADS):
        lo, hi = h * FOX_DH, (h + 1) * FOX_DH
        x = p_ref[:, lo:hi]
        y = x * lax.rsqrt(jnp.mean(x * x, axis=-1, keepdims=True) + RMS_EPS) * gq_ref[...]
        q_ref[:, lo:hi] = (y * scale).astype(q_ref.dtype)
        x = p_ref[:, FOX_W + lo:FOX_W + hi]
        k_ref[:, lo:hi] = x * lax.rsqrt(jnp.mean(x * x, axis=-1, keepdims=True) + RMS_EPS) * gk_ref[...]


def fox_norm(p, g_q, g_k):
    t = p.shape[0]
    tm = _pick(t, (256, 128))
    return pl.pallas_call(
        _fox_norm_kernel,
        grid=(t // tm,),
        in_specs=[pl.BlockSpec((tm, 2 * FOX_W), lambda i: (i, 0)),
                  pl.BlockSpec((1, FOX_DH), lambda i: (0, 0)), pl.BlockSpec((1, FOX_DH), lambda i: (0, 0))],
        out_specs=[pl.BlockSpec((tm, FOX_W), lambda i: (i, 0))] * 2,
        out_shape=[jax.ShapeDtypeStruct((t, FOX_W), BF16), jax.ShapeDtypeStruct((t, FOX_W), F32)],
        compiler_params=_params("parallel"),
        name="fox_norm",
    )(p, g_q.reshape(1, FOX_DH), g_k.reshape(1, FOX_DH))


def _fox_prompt_kernel(qi_ref, ki_ref, q_ref, k_ref, v_ref, f_ref, o_ref, m_ref, l_ref, acc_ref, *, tq, tk):
    n = pl.program_id(2)
    i = qi_ref[n]
    kk = ki_ref[n]

    @pl.when(kk == 0)
    def _():
        m_ref[...] = jnp.full_like(m_ref, NEG_INF)
        l_ref[...] = jnp.zeros_like(l_ref)
        acc_ref[...] = jnp.zeros_like(acc_ref)

    def step(masked):
        s = lax.dot_general(q_ref[...], k_ref[...].astype(BF16), (((1,), (1,)), ((), ())),
                            preferred_element_type=F32)
        s = s - f_ref[0]
        if masked:
            q_pos = i * tq + lax.broadcasted_iota(jnp.int32, s.shape, 0)
            k_pos = kk * tk + lax.broadcasted_iota(jnp.int32, s.shape, 1)
            s = jnp.where(q_pos >= k_pos, s, NEG_INF)
        m_prev = m_ref[...]
        m_new = jnp.maximum(m_prev, jnp.max(s, axis=1, keepdims=True))
        alpha = jnp.exp(m_prev - m_new)
        p = jnp.exp(s - m_new)
        l_ref[...] = alpha * l_ref[...] + jnp.sum(p, axis=1, keepdims=True)
        acc_ref[...] = alpha * acc_ref[...] + jnp.dot(p.astype(BF16), v_ref[...].astype(BF16),
                                                      preferred_element_type=F32)
        m_ref[...] = m_new

    on_diag = (kk + 1) * tk > i * tq + 1

    @pl.when(on_diag)
    def _():
        step(True)

    @pl.when(jnp.logical_not(on_diag))
    def _():
        step(False)

    @pl.when(kk == (i * tq + tq - 1) // tk)
    def _():
        o_ref[...] = (acc_ref[...] / l_ref[...]).astype(o_ref.dtype)


def fox_prompt(q_bf, k, p, f_cum, bsz, s_len, tq=1024, tk=1024):
    nq, nk = s_len // tq, s_len // tk
    pairs = [(i, kk) for i in range(nq) for kk in range((i * tq + tq - 1) // tk + 1)]
    qi = jnp.asarray([a for a, _ in pairs], jnp.int32)
    ki = jnp.asarray([b for _, b in pairs], jnp.int32)
    v_col0 = 2 * FOX_W // FOX_DH
    grid_spec = pltpu.PrefetchScalarGridSpec(
        num_scalar_prefetch=2,
        grid=(bsz, FOX_HEADS, len(pairs)),
        in_specs=[pl.BlockSpec((tq, FOX_DH), lambda b, h, n, qi, ki: (b * nq + qi[n], h)),
                  pl.BlockSpec((tk, FOX_DH), lambda b, h, n, qi, ki: (b * nk + ki[n], h)),
                  pl.BlockSpec((tk, FOX_DH), lambda b, h, n, qi, ki: (b * nk + ki[n], v_col0 + h)),
                  pl.BlockSpec((1, 1, tk), lambda b, h, n, qi, ki: (b * FOX_HEADS + h, 0, ki[n]))],
        out_specs=pl.BlockSpec((tq, FOX_DH), lambda b, h, n, qi, ki: (b * nq + qi[n], h)),
        scratch_shapes=[pltpu.VMEM((tq, 1), F32), pltpu.VMEM((tq, 1), F32), pltpu.VMEM((tq, FOX_DH), F32)],
    )
    return pl.pallas_call(
        functools.partial(_fox_prompt_kernel, tq=tq, tk=tk),
        grid_spec=grid_spec,
        out_shape=jax.ShapeDtypeStruct((bsz * s_len, FOX_W), BF16),
        compiler_params=_params("parallel", "parallel", "arbitrary"),
        name="fox_prompt",
    )(qi, ki, q_bf, k, p, f_cum)


SUB = 8


def _logf_cumsum_kernel(pt_ref, *refs):
    lf_refs, o_ref = refs[:-1], refs[-1]
    page, nh = lf_refs[0].shape[1:]
    tri = (lax.broadcasted_iota(jnp.int32, (page, page), 1)
           <= lax.broadcasted_iota(jnp.int32, (page, page), 0)).astype(F32)
    carry = jnp.zeros((1, nh), F32)
    for p, lf_ref in enumerate(lf_refs):
        f = jnp.dot(tri, lf_ref[0], precision=lax.Precision.HIGHEST, preferred_element_type=F32) + carry
        carry = f[page - 1:page, :]
        o_ref[0, p] = f


def paged_logf_cumsum(cache_logf, page_table):
    bsz, n_pages = page_table.shape
    _, page, nh = cache_logf.shape

    def page_spec(p):
        return pl.BlockSpec((1, page, nh), lambda b, pt: (pt[b, p], 0, 0))

    grid_spec = pltpu.PrefetchScalarGridSpec(
        num_scalar_prefetch=1,
        grid=(bsz,),
        in_specs=[page_spec(p) for p in range(n_pages)],
        out_specs=pl.BlockSpec((1, n_pages, page, nh), lambda b, pt: (b, 0, 0, 0)),
    )
    return pl.pallas_call(
        _logf_cumsum_kernel,
        grid_spec=grid_spec,
        out_shape=jax.ShapeDtypeStruct((bsz, n_pages, page, nh), F32),
        compiler_params=_params("parallel"),
        name="logf_cumsum",
    )(page_table, *([cache_logf] * n_pages))


def _fox_sample_kernel(pt_ref, q_ref, *refs, n_groups, group):
    kc_refs, vc_refs = refs[:group], refs[group:2 * group]
    f_ref, mask_ref, kn_ref, vn_ref, fn_ref, maskn_ref, o_ref, m_ref, l_ref, acc_ref = refs[2 * group:]
    pg = pl.program_id(1)

    @pl.when(pg == 0)
    def _():
        m_ref[...] = jnp.full_like(m_ref, NEG_INF)
        l_ref[...] = jnp.zeros_like(l_ref)
        acc_ref[...] = jnp.zeros_like(acc_ref)

    def attend(kvb):
        ss = [lax.dot_general(q_ref[0], k2, (((1,), (1,)), ((), ())), preferred_element_type=F32) + b
              for k2, _, b in kvb]
        m_prev = m_ref[...]
        m_new = m_prev
        for s in ss:
            m_new = jnp.maximum(m_new, jnp.max(s, axis=1, keepdims=True))
        alpha = jnp.exp(m_prev - m_new)
        l_new = alpha * l_ref[...]
        acc = alpha * acc_ref[...]
        for s, (_, v2, _) in zip(ss, kvb):
            p = jnp.exp(s - m_new)
            l_new = l_new + jnp.sum(p, axis=1, keepdims=True)
            acc = acc + jnp.dot(p.astype(BF16), v2, preferred_element_type=F32)
        l_ref[...] = l_new
        acc_ref[...] = acc
        m_ref[...] = m_new

    @pl.when(pg < n_groups)
    def _():
        rows = kc_refs[0].shape[0] * kc_refs[0].shape[1]
        attend([(kc[...].reshape(rows, FOX_DH).astype(BF16), vc[...].reshape(rows, FOX_DH).astype(BF16),
                 mask_ref[...] - f_ref[0, g])
                for g, (kc, vc) in enumerate(zip(kc_refs, vc_refs))])

    @pl.when(pg == n_groups)
    def _():
        attend([(kn_ref[0].astype(BF16), vn_ref[0].astype(BF16), maskn_ref[...] - fn_ref[0])])
        o_ref[0] = acc_ref[...] / l_ref[...]


def fox_sample(q_bf, k_new, v_new, lf_new, cache_k, cache_v, cache_logf, page_table):
    bsz, t_len, _ = q_bf.shape
    n_pool, page = cache_k.shape[:2]
    n_pages = page_table.shape[1]
    nh, dh = FOX_HEADS, FOX_DH
    rows = nh * t_len
    lanes = page * nh
    lanes_new = SUB * nh
    qf = q_bf.reshape(bsz, t_len, nh, dh).transpose(0, 2, 1, 3).reshape(bsz, rows, dh)
    f_pages = paged_logf_cumsum(cache_logf, page_table)
    f_rows = f_pages.reshape(bsz, n_pages, 1, lanes)
    pad = ((0, 0), (0, SUB - t_len), (0, 0))
    kn = jnp.pad(k_new, pad).reshape(bsz, lanes_new, dh)
    vn = jnp.pad(v_new, pad).reshape(bsz, lanes_new, dh)
    fn = (f_pages[:, -1, -1, :][:, None, :] + jnp.cumsum(jnp.pad(lf_new, pad), axis=1)).reshape(bsz, 1, lanes_new)
    row_head, row_q = np.arange(rows) // t_len, np.arange(rows) % t_len
    mask = np.where(row_head[:, None] == (np.arange(lanes) % nh)[None, :], 0.0, NEG_INF).astype(np.float32)
    key_n, head_n = np.arange(lanes_new) // nh, np.arange(lanes_new) % nh
    ok = (row_head[:, None] == head_n[None, :]) & (key_n[None, :] <= row_q[:, None]) & (key_n[None, :] < t_len)
    mask_new = np.where(ok, 0.0, NEG_INF).astype(np.float32)
    group = _pick(n_pages, (4, 2, 1))
    n_groups = n_pages // group
    last = n_groups - 1

    def cache_spec(g):
        return pl.BlockSpec((page, nh, dh), lambda b, p, pt: (pt[b, jnp.minimum(p, last) * group + g], 0, 0))

    cache_specs = [cache_spec(g) for g in range(group)]
    kc3 = cache_k.reshape(n_pool * page, nh, dh)
    vc3 = cache_v.reshape(n_pool * page, nh, dh)
    grid_spec = pltpu.PrefetchScalarGridSpec(
        num_scalar_prefetch=1,
        grid=(bsz, n_groups + 1),
        in_specs=[pl.BlockSpec((1, rows, dh), lambda b, p, pt: (b, 0, 0)),
                  *cache_specs, *cache_specs,
                  pl.BlockSpec((1, group, 1, lanes), lambda b, p, pt: (b, jnp.minimum(p, last), 0, 0)),
                  pl.BlockSpec((rows, lanes), lambda b, p, pt: (0, 0)),
                  pl.BlockSpec((1, lanes_new, dh), lambda b, p, pt: (b, 0, 0)),
                  pl.BlockSpec((1, lanes_new, dh), lambda b, p, pt: (b, 0, 0)),
                  pl.BlockSpec((1, 1, lanes_new), lambda b, p, pt: (b, 0, 0)),
                  pl.BlockSpec((rows, lanes_new), lambda b, p, pt: (0, 0))],
        out_specs=pl.BlockSpec((1, rows, dh), lambda b, p, pt: (b, 0, 0)),
        scratch_shapes=[pltpu.VMEM((rows, 1), F32), pltpu.VMEM((rows, 1), F32), pltpu.VMEM((rows, dh), F32)],
    )
    out = pl.pallas_call(
        functools.partial(_fox_sample_kernel, n_groups=n_groups, group=group),
        grid_spec=grid_spec,
        out_shape=jax.ShapeDtypeStruct((bsz, rows, dh), F32),
        compiler_params=_params("parallel", "arbitrary"),
        name="fox_sample",
    )(page_table, qf, *([kc3] * group), *([vc3] * group),
      f_rows, jnp.asarray(mask), kn, vn, fn, jnp.asarray(mask_new))
    return out.reshape(bsz, nh, t_len, dh).transpose(0, 2, 1, 3).reshape(bsz, t_len, nh * dh)


def _mlstm_kernel(q_ref, k_ref, v_ref, o_ref, g_ref, c0_ref, n0_ref, m0_ref, mg_ref,
                  h_ref, c1_ref, n1_ref, m1_ref, c_scr, n_scr, m_scr):
    c = pl.program_id(1)
    L = q_ref.shape[0]
    nh, dh = MLSTM_HEADS, MLSTM_DH

    @pl.when(c == 0)
    def _():
        c_scr[...] = c0_ref[0]
        n_scr[...] = n0_ref[0]
        m_scr[...] = m0_ref[0]

    t_idx = lax.broadcasted_iota(jnp.int32, (L, L), 0)
    s_idx = lax.broadcasted_iota(jnp.int32, (L, L), 1)
    causal = s_idx <= t_idx
    diag = s_idx == t_idx
    for h in range(nh):
        cols = slice(h * dh, (h + 1) * dh)
        ig = g_ref[0, h:h + 1, :]
        lf = g_ref[0, nh + h:nh + h + 1, :]
        m0 = m_scr[h:h + 1, :]
        f_col = jnp.sum(jnp.where(causal, lf, 0.0), axis=1, keepdims=True)
        ig_col = jnp.sum(jnp.where(diag, ig, 0.0), axis=1, keepdims=True)
        g_col = ig_col - f_col
        g_row = jnp.sum(jnp.where(diag, g_col, 0.0), axis=0, keepdims=True)
        cm_col = jnp.max(jnp.where(causal, g_row, NEG_INF), axis=1, keepdims=True)
        m_col = f_col + jnp.maximum(m0, cm_col)
        b_col = f_col - m_col
        a_col = jnp.exp(f_col + m0 - m_col)
        d = jnp.exp(jnp.where(causal, b_col + g_row, NEG_INF))
        q = q_ref[:, cols] * (dh ** -0.5)
        k = k_ref[:, cols]
        v = v_ref[:, cols]
        qb, kb, vb = q.astype(BF16), k.astype(BF16), v.astype(BF16)
        qk = lax.dot_general(qb, kb, (((1,), (1,)), ((), ())), preferred_element_type=F32) * d
        c_old = c_scr[h]
        n_old = n_scr[h:h + 1, :]
        num = a_col * jnp.dot(qb, c_old.astype(BF16), preferred_element_type=F32) \
            + jnp.dot(qk.astype(BF16), vb, preferred_element_type=F32)
        den = a_col * jnp.sum(q * n_old, axis=1, keepdims=True) + jnp.sum(qk, axis=1, keepdims=True)
        hs = num / jnp.maximum(jnp.abs(den), jnp.exp(-m_col))
        hn = hs * lax.rsqrt(jnp.mean(hs * hs, axis=1, keepdims=True) + RMS_EPS) * mg_ref[:, cols]
        h_ref[:, cols] = (hn * jax.nn.sigmoid(o_ref[:, cols])).astype(h_ref.dtype)
        b_last = b_col[L - 1:L, :]
        a_last = a_col[L - 1:L, :]
        kd = k * jnp.exp(b_last + g_col)
        c_scr[h] = a_last * c_old + lax.dot_general(kd.astype(BF16), vb, (((0,), (0,)), ((), ())),
                                                     preferred_element_type=F32)
        n_scr[h:h + 1, :] = a_last * n_old + jnp.sum(kd, axis=0, keepdims=True)
        m_scr[h:h + 1, :] = m_col[L - 1:L, :]

    @pl.when(c == pl.num_programs(1) - 1)
    def _():
        c1_ref[0] = c_scr[...]
        n1_ref[0] = n_scr[...]
        m1_ref[0] = m_scr[...]


def mlstm(p, gates, c0, n0, m0, mh_g, bsz, n_chunks, chunk, col0):
    nh, dh, mw = MLSTM_HEADS, MLSTM_DH, MLSTM_W
    rows = bsz * n_chunks * chunk

    def col_spec(j):
        return pl.BlockSpec((chunk, mw), lambda b, c: (b * n_chunks + c, col0 + j))

    state = lambda b, c: (b, 0, 0)
    return pl.pallas_call(
        _mlstm_kernel,
        grid=(bsz, n_chunks),
        in_specs=[col_spec(0), col_spec(1), col_spec(2), col_spec(3),
                  pl.BlockSpec((1, 2 * nh, chunk), lambda b, c: (b * n_chunks + c, 0, 0)),
                  pl.BlockSpec((1, nh, dh, dh), lambda b, c: (b, 0, 0, 0)),
                  pl.BlockSpec((1, nh, dh), state), pl.BlockSpec((1, nh, 1), state),
                  pl.BlockSpec((1, mw), lambda b, c: (0, 0))],
        out_specs=[pl.BlockSpec((chunk, mw), lambda b, c: (b * n_chunks + c, 0)),
                   pl.BlockSpec((1, nh, dh, dh), lambda b, c: (b, 0, 0, 0)),
                   pl.BlockSpec((1, nh, dh), state), pl.BlockSpec((1, nh, 1), state)],
        out_shape=[jax.ShapeDtypeStruct((rows, mw), BF16),
                   jax.ShapeDtypeStruct((bsz, nh, dh, dh), F32),
                   jax.ShapeDtypeStruct((bsz, nh, dh), F32),
                   jax.ShapeDtypeStruct((bsz, nh, 1), F32)],
        scratch_shapes=[pltpu.VMEM((nh, dh, dh), F32), pltpu.VMEM((nh, dh), F32), pltpu.VMEM((nh, 1), F32)],
        compiler_params=_params("parallel", "arbitrary"),
        name="mlstm",
    )(p, p, p, p, gates, c0, n0, m0.reshape(bsz, nh, 1), mh_g.reshape(1, mw))


CONV_HALO = 32
CONV_ROWS = 32


def _group_ln_swish(y, g, b):
    outs = []
    for grp in range(CONV_GROUPS):
        x = y[:, grp * LANES:(grp + 1) * LANES]
        mu = jnp.mean(x, axis=1, keepdims=True)
        xc = x - mu
        var = jnp.mean(xc * xc, axis=1, keepdims=True)
        z = xc * lax.rsqrt(var + LN_EPS) * g[:, grp * LANES:(grp + 1) * LANES] + b[:, grp * LANES:(grp + 1) * LANES]
        outs.append(z * jax.nn.sigmoid(z))
    return jnp.concatenate(outs, axis=1)


def _conv_prompt_kernel(cv_ref, cg_ref, hv_ref, hg_ref, w_ref, b_ref, g_ref, beta_ref, y_ref, buf_ref, ext_ref):
    i = pl.program_id(1)
    ts = cv_ref.shape[0]
    hist = jnp.where(i > 0, hv_ref[...] * jax.nn.sigmoid(hg_ref[...]), 0.0)
    ext_ref[0:CONV_HALO, :] = hist
    ext_ref[CONV_HALO:, :] = cv_ref[...] * jax.nn.sigmoid(cg_ref[...])
    off = CONV_HALO - (CONV_K - 1)
    for r0 in range(0, ts, CONV_ROWS):
        acc = jnp.zeros((CONV_ROWS, CONV_W), F32) + b_ref[...]
        for j in range(CONV_K):
            acc = acc + ext_ref[r0 + off + j:r0 + off + j + CONV_ROWS, :] * w_ref[j:j + 1, :]
        y_ref[r0:r0 + CONV_ROWS, :] = _group_ln_swish(acc, g_ref[...], beta_ref[...]).astype(y_ref.dtype)

    @pl.when(i == pl.num_programs(1) - 1)
    def _():
        buf_ref[0] = ext_ref[ts + off:ts + CONV_HALO, :]


def conv_prompt(p, conv_w, conv_b, cn_g, cn_b, bsz, s_len, ts=512):
    nt = s_len // ts
    hb = ts // CONV_HALO

    def halo(col):
        return pl.BlockSpec((CONV_HALO, CONV_W), lambda b, i: (jnp.maximum((b * nt + i) * hb - 1, 0), col))

    vec = pl.BlockSpec((1, CONV_W), lambda b, i: (0, 0))
    return pl.pallas_call(
        _conv_prompt_kernel,
        grid=(bsz, nt),
        in_specs=[pl.BlockSpec((ts, CONV_W), lambda b, i: (b * nt + i, 0)),
                  pl.BlockSpec((ts, CONV_W), lambda b, i: (b * nt + i, 1)),
                  halo(0), halo(1),
                  pl.BlockSpec((CONV_K, CONV_W), lambda b, i: (0, 0)), vec, vec, vec],
        out_specs=[pl.BlockSpec((ts, CONV_W), lambda b, i: (b * nt + i, 0)),
                   pl.BlockSpec((1, CONV_K - 1, CONV_W), lambda b, i: (b, 0, 0))],
        out_shape=[jax.ShapeDtypeStruct((bsz * s_len, CONV_W), BF16),
                   jax.ShapeDtypeStruct((bsz, CONV_K - 1, CONV_W), F32)],
        scratch_shapes=[pltpu.VMEM((ts + CONV_HALO, CONV_W), F32)],
        compiler_params=_params("parallel", "arbitrary"),
        name="conv_prompt",
    )(p, p, p, p, conv_w, conv_b.reshape(1, -1), cn_g.reshape(1, -1), cn_b.reshape(1, -1))


def _conv_sample_kernel(cv_ref, cg_ref, st_ref, w_ref, b_ref, g_ref, beta_ref, y_ref, buf_ref, ext_ref, *, t_len):
    nb = st_ref.shape[0]
    hist = CONV_K - 1
    y_ref[...] = jnp.zeros_like(y_ref)
    for bb in range(nb):
        rows = slice(bb * SUB, bb * SUB + t_len)
        ext_ref[0:hist, :] = st_ref[bb]
        ext_ref[hist:hist + t_len, :] = cv_ref[rows, :] * jax.nn.sigmoid(cg_ref[rows, :])
        ys = [jnp.sum(ext_ref[t:t + CONV_K, :] * w_ref[...], axis=0, keepdims=True) for t in range(t_len)]
        y = jnp.concatenate(ys, axis=0) + b_ref[...]
        y_ref[rows, :] = _group_ln_swish(y, g_ref[...], beta_ref[...]).astype(y_ref.dtype)
        buf_ref[bb] = ext_ref[t_len:t_len + hist, :]


def conv_sample(p_pad, state_conv, conv_w, conv_b, cn_g, cn_b, t_len, nb=8):
    bsz = state_conv.shape[0]
    vec = pl.BlockSpec((1, CONV_W), lambda i: (0, 0))
    return pl.pallas_call(
        functools.partial(_conv_sample_kernel, t_len=t_len),
        grid=(bsz // nb,),
        in_specs=[pl.BlockSpec((nb * SUB, CONV_W), lambda i: (i, 0)),
                  pl.BlockSpec((nb * SUB, CONV_W), lambda i: (i, 1)),
                  pl.BlockSpec((nb, CONV_K - 1, CONV_W), lambda i: (i, 0, 0)),
                  pl.BlockSpec((CONV_K, CONV_W), lambda i: (0, 0)), vec, vec, vec],
        out_specs=[pl.BlockSpec((nb * SUB, CONV_W), lambda i: (i, 0)),
                   pl.BlockSpec((nb, CONV_K - 1, CONV_W), lambda i: (i, 0, 0))],
        out_shape=[jax.ShapeDtypeStruct((bsz * SUB, CONV_W), BF16),
                   jax.ShapeDtypeStruct((bsz, CONV_K - 1, CONV_W), F32)],
        scratch_shapes=[pltpu.VMEM((CONV_K - 1 + SUB, CONV_W), F32)],
        compiler_params=_params("parallel"),
        name="conv_sample",
    )(p_pad, p_pad, state_conv, conv_w, conv_b.reshape(1, -1), cn_g.reshape(1, -1), cn_b.reshape(1, -1))


def kernel(x_prompt, x_sample, state_conv, state_C, state_n, state_m, cache_k, cache_v, cache_logf, page_table,
           norm0_mix, w_in0, b_ig0, b_fg0, conv_w0, conv_b0, conv_norm_g0, conv_norm_b0, mlstm_norm_g0, w_out0,
           norm1_mix, w_in1, b_f1, q_norm_g1, k_norm_g1, w_out1,
           norm0_ffn, peer_wq0, peer_keys0, peer_u0, peer_v0,
           norm1_ffn, peer_wq1, peer_keys1, peer_u1, peer_v1):
    bp, sp, d = x_prompt.shape
    bs, ss, _ = x_sample.shape
    tp, ts = bp * sp, bs * ss
    x = jnp.concatenate([x_prompt.reshape(tp, d), x_sample.reshape(ts, d)], axis=0)

    n_main0 = 2 * CONV_W + 4 * MLSTM_W
    xn = rmsnorm_bf16(x, norm0_mix)
    p0 = mm_tokens(xn, w_in0[:, :n_main0].astype(BF16))
    g0 = gate_proj(x, norm0_mix, w_in0[:, n_main0:])
    nh = MLSTM_HEADS
    gates = jnp.concatenate([g0[:, :nh] + b_ig0, jax.nn.log_sigmoid(g0[:, nh:] + b_fg0)], axis=1)
    conv_par = (conv_w0, conv_b0, conv_norm_g0, conv_norm_b0)
    mcol0 = 2 * CONV_W // MLSTM_W
    nc = sp // MLSTM_CHUNK
    gates_p = gates[:tp].reshape(bp * nc, MLSTM_CHUNK, 2 * nh).transpose(0, 2, 1)
    hp, p_C, p_n, p_m = mlstm(p0, gates_p, jnp.zeros((bp, nh, MLSTM_DH, MLSTM_DH), F32),
                              jnp.zeros((bp, nh, MLSTM_DH), F32), jnp.zeros((bp, nh), F32),
                              mlstm_norm_g0, bp, nc, MLSTM_CHUNK, mcol0)
    yp, p_conv = conv_prompt(p0, *conv_par, bp, sp)
    tpad = ((0, 0), (0, SUB - ss), (0, 0))
    p0s = jnp.pad(p0[tp:].reshape(bs, ss, -1), tpad).reshape(bs * SUB, -1)
    gs = gates[tp:].reshape(bs, ss, 2 * nh)
    gates_s = jnp.concatenate([jnp.pad(gs[..., :nh], tpad, constant_values=NEG_INF),
                               jnp.pad(gs[..., nh:], tpad)], axis=-1).transpose(0, 2, 1)
    hs, s_C, s_n, s_m = mlstm(p0s, gates_s, state_C, state_n, state_m, mlstm_norm_g0, bs, 1, SUB, mcol0)
    ys, s_conv = conv_sample(p0s, state_conv, *conv_par, ss)
    unpad = lambda a: a.reshape(bs, SUB, -1)[:, :ss].reshape(ts, -1)
    y_all = jnp.concatenate([yp, unpad(ys)], axis=0)
    h_all = jnp.concatenate([hp[:tp], unpad(hs)], axis=0)
    w_out0_bf = w_out0.astype(BF16)
    x = x + mm2(y_all, h_all, w_out0_bf[:CONV_W], w_out0_bf[CONV_W:], _pick(tp + ts, (512, 256, 128)), 1024)
    p_m, s_m = p_m.reshape(bp, nh), s_m.reshape(bs, nh)
    x = peer(x, norm0_ffn, peer_wq0, peer_keys0, peer_u0, peer_v0)

    xn = rmsnorm_bf16(x, norm1_mix)
    p1 = mm_tokens(xn, w_in1[:, :3 * FOX_W].astype(BF16))
    g1 = gate_proj(x, norm1_mix, w_in1[:, 3 * FOX_W:])
    lf = jax.nn.log_sigmoid(g1 + b_f1)
    q_bf, k = fox_norm(p1, q_norm_g1, k_norm_g1)
    v = p1[:, 2 * FOX_W:]
    f_cum = jnp.cumsum(lf[:tp].reshape(bp, sp, FOX_HEADS), axis=1).transpose(0, 2, 1).reshape(bp * FOX_HEADS, 1, sp)
    attn_p = fox_prompt(q_bf, k, p1, f_cum, bp, sp)
    attn_s = fox_sample(q_bf[tp:].reshape(bs, ss, FOX_W), k[tp:].reshape(bs, ss, FOX_W),
                        v[tp:].reshape(bs, ss, FOX_W), lf[tp:].reshape(bs, ss, FOX_HEADS),
                        cache_k, cache_v, cache_logf, page_table)
    attn = jnp.concatenate([attn_p, attn_s.reshape(ts, FOX_W).astype(BF16)], axis=0)
    x = x + mm_tokens(attn, w_out1.astype(BF16))
    x = peer(x, norm1_ffn, peer_wq1, peer_keys1, peer_u1, peer_v1)

    hd = (FOX_HEADS, FOX_DH)
    return (x[:tp].reshape(bp, sp, d), x[tp:].reshape(bs, ss, d), p_conv, p_C, p_n, p_m,
            k[:tp].reshape(bp, sp, *hd), v[:tp].reshape(bp, sp, *hd), lf[:tp].reshape(bp, sp, FOX_HEADS),
            s_conv, s_C, s_n, s_m,
            k[tp:].reshape(bs, ss, *hd), v[tp:].reshape(bs, ss, *hd), lf[tp:].reshape(bs, ss, FOX_HEADS))
```

```python
import functools
import math

import jax
import jax.numpy as jnp
import numpy as np
from jax import lax
from jax.experimental import pallas as pl
from jax.experimental.pallas import tpu as pltpu

F32 = jnp.float32
BF16 = jnp.bfloat16

D_MODEL = 2048
CONV_W = 1024
CONV_K = 31
CONV_GROUPS = 8
MLSTM_HEADS = 4
MLSTM_W = 1024
MLSTM_DH = 256
MLSTM_CHUNK = 128
FOX_HEADS = 16
FOX_DH = 128
FOX_W = 2048
PEER_HEADS = 8
PEER_NKEYS = 128
PEER_DK = 128
PEER_TOPK = 16
RMS_EPS = 1e-6
LN_EPS = 1e-5

LANES = 128
VMEM_LIMIT = 56 * 1024 * 1024
NEG_INF = float("-inf")


def _params(*sem):
    return pltpu.CompilerParams(dimension_semantics=sem, vmem_limit_bytes=VMEM_LIMIT)


def _rmsnorm_kernel(x_ref, g_ref, o_ref):
    x = x_ref[...]
    y = x * lax.rsqrt(jnp.mean(x * x, axis=-1, keepdims=True) + RMS_EPS)
    o_ref[...] = (y * g_ref[...]).astype(o_ref.dtype)


def rmsnorm_bf16(x, g):
    t, d = x.shape
    tm = _pick(t, (512, 256, 128))
    return pl.pallas_call(
        _rmsnorm_kernel,
        grid=(t // tm,),
        in_specs=[pl.BlockSpec((tm, d), lambda i: (i, 0)), pl.BlockSpec((1, d), lambda i: (0, 0))],
        out_specs=pl.BlockSpec((tm, d), lambda i: (i, 0)),
        out_shape=jax.ShapeDtypeStruct((t, d), BF16),
        compiler_params=_params("parallel"),
        name="rmsnorm",
    )(x, g.reshape(1, d))


def _rmsnorm_t_kernel(x_ref, g_ref, o_ref):
    x = x_ref[...]
    y = x * lax.rsqrt(jnp.mean(x * x, axis=-1, keepdims=True) + RMS_EPS) * g_ref[...]
    o_ref[...] = y.T.astype(o_ref.dtype)


def rmsnorm_bf16_t(x, g):
    t, d = x.shape
    tm = _pick(t, (512, 256, 128))
    return pl.pallas_call(
        _rmsnorm_t_kernel,
        grid=(t // tm,),
        in_specs=[pl.BlockSpec((tm, d), lambda i: (i, 0)), pl.BlockSpec((1, d), lambda i: (0, 0))],
        out_specs=pl.BlockSpec((d, tm), lambda i: (0, i)),
        out_shape=jax.ShapeDtypeStruct((d, t), BF16),
        compiler_params=_params("parallel"),
        name="rmsnorm_t",
    )(x, g.reshape(1, d))


def _cast_t_kernel(x_ref, o_ref):
    o_ref[...] = x_ref[...].T.astype(o_ref.dtype)


def cast_transposed_bf16(w):
    n, d = w.shape
    tn = _pick(n, (512, 256, 128))
    return pl.pallas_call(
        _cast_t_kernel,
        grid=(n // tn,),
        in_specs=[pl.BlockSpec((tn, d), lambda i: (i, 0))],
        out_specs=pl.BlockSpec((d, tn), lambda i: (0, i)),
        out_shape=jax.ShapeDtypeStruct((d, n), BF16),
        compiler_params=_params("parallel"),
        name="cast_t",
    )(w)


def _add_t_kernel(x_ref, yt_ref, o_ref):
    o_ref[...] = x_ref[...] + yt_ref[...].T


def add_transposed(x, y_t):
    t, d = x.shape
    tm = _pick(t, (512, 256, 128))
    return pl.pallas_call(
        _add_t_kernel,
        grid=(t // tm,),
        in_specs=[pl.BlockSpec((tm, d), lambda i: (i, 0)), pl.BlockSpec((d, tm), lambda i: (0, i))],
        out_specs=pl.BlockSpec((tm, d), lambda i: (i, 0)),
        out_shape=jax.ShapeDtypeStruct((t, d), F32),
        compiler_params=_params("parallel"),
        name="add_t",
    )(x, y_t)


def _gate_proj_kernel(x_ref, g_ref, w_ref, o_ref):
    x = x_ref[...]
    y = x * lax.rsqrt(jnp.mean(x * x, axis=-1, keepdims=True) + RMS_EPS) * g_ref[...]
    o_ref[...] = jnp.dot(y, w_ref[...], precision=lax.Precision.HIGHEST, preferred_element_type=F32)


def gate_proj(x, g, w):
    t, d = x.shape
    n = w.shape[1]
    tm = _pick(t, (256, 128))
    out = pl.pallas_call(
        _gate_proj_kernel,
        grid=(t // tm,),
        in_specs=[pl.BlockSpec((tm, d), lambda i: (i, 0)), pl.BlockSpec((1, d), lambda i: (0, 0)),
                  pl.BlockSpec((d, LANES), lambda i: (0, 0))],
        out_specs=pl.BlockSpec((tm, LANES), lambda i: (i, 0)),
        out_shape=jax.ShapeDtypeStruct((t, LANES), F32),
        compiler_params=_params("parallel"),
        name="gate_proj",
    )(x, g.reshape(1, d), jnp.pad(w, ((0, 0), (0, LANES - n))))
    return out[:, :n]


def _mm_kernel(x_ref, w_ref, o_ref):
    o_ref[...] = jnp.dot(x_ref[...], w_ref[...], preferred_element_type=F32).astype(o_ref.dtype)


def mm(x, w, tm, tn, out_dtype=F32):
    m, k = x.shape
    n = w.shape[1]
    assert m % tm == 0 and n % tn == 0, (m, n, tm, tn)
    return pl.pallas_call(
        _mm_kernel,
        grid=(n // tn, m // tm),
        in_specs=[pl.BlockSpec((tm, k), lambda j, i: (i, 0)), pl.BlockSpec((k, tn), lambda j, i: (0, j))],
        out_specs=pl.BlockSpec((tm, tn), lambda j, i: (i, j)),
        out_shape=jax.ShapeDtypeStruct((m, n), out_dtype),
        compiler_params=_params("parallel", "parallel"),
        name="mm",
    )(x, w)


def _mm_res_kernel(r_ref, *refs):
    o_ref = refs[-1]
    n_ops = (len(refs) - 1) // 2
    acc = r_ref[...]
    for x_ref, w_ref in zip(refs[:n_ops], refs[n_ops:2 * n_ops]):
        acc = acc + jnp.dot(x_ref[...], w_ref[...], preferred_element_type=F32)
    o_ref[...] = acc


def mm_residual(r, xs, ws, tm, tn):
    m, n = r.shape
    return pl.pallas_call(
        _mm_res_kernel,
        grid=(n // tn, m // tm),
        in_specs=[pl.BlockSpec((tm, tn), lambda j, i: (i, j))]
        + [pl.BlockSpec((tm, x.shape[1]), lambda j, i: (i, 0)) for x in xs]
        + [pl.BlockSpec((w.shape[0], tn), lambda j, i: (0, j)) for w in ws],
        out_specs=pl.BlockSpec((tm, tn), lambda j, i: (i, j)),
        out_shape=jax.ShapeDtypeStruct((m, n), F32),
        compiler_params=_params("parallel", "parallel"),
        name="mm_res",
    )(r, *xs, *ws)


def _pick(n, cands):
    for c in cands:
        if n % c == 0:
            return c
    raise ValueError(n)


def mm_tokens(x, w, out_dtype=F32):
    t = x.shape[0]
    n = w.shape[1]
    return mm(x, w, _pick(t, (512, 256, 128)), _pick(n, (1024, 512, 256, 128)), out_dtype)


def _topk_rows(arrays, k):
    n, lanes = arrays[0].shape
    iota = lax.broadcasted_iota(jnp.int32, (n, lanes), 0)
    iota_k = lax.broadcasted_iota(jnp.int32, (k, lanes), 0)
    unranked = jnp.full((n, lanes), k, jnp.int32)
    zeros_k = jnp.zeros((k, lanes), F32)

    def run(exact_ties):
        carries_rank = [exact_ties or i == len(arrays) - 1 for i in range(len(arrays))]

        def body(r, carry):
            out = []
            for (w, rank, vals), has_rank in zip(carry, carries_rank):
                mx = jnp.max(w, axis=0, keepdims=True)
                hit = w == mx
                if exact_ties:
                    hit = iota == jnp.min(jnp.where(hit, iota, n), axis=0, keepdims=True)
                out.append((jnp.where(hit, NEG_INF, w), jnp.where(hit, r, rank) if has_rank else rank,
                            jnp.where(iota_k == r, mx, vals)))
            return tuple(out)

        init = tuple((s, unranked if has_rank else 0, zeros_k) for s, has_rank in zip(arrays, carries_rank))
        res = []
        for s, (_, rank, vals), has_rank in zip(arrays, lax.fori_loop(0, k, body, init), carries_rank):
            if not has_rank:
                rank = unranked
                for r in range(k):
                    rank = jnp.where(s == vals[r:r + 1, :], r, rank)
            res.append((rank, vals))
        return tuple(res)

    fast = run(False)
    ranked = sum(jnp.sum((rank < k).astype(jnp.int32), axis=0, keepdims=True) for rank, _ in fast)
    clean = jnp.max(ranked) == k * len(arrays)
    res = lax.cond(clean, lambda: fast, lambda: run(True))
    return [(vals, rank) for rank, vals in res]


def _peer_select_kernel(s_ref, rank2_ref, e2z_ref, nsel_ref, e1_ref):
    k = PEER_TOPK
    half = k // 2
    L = s_ref.shape[1]
    sub = lax.broadcasted_iota(jnp.int32, (half, L), 0)

    def head(h, _):
        base = pl.multiple_of(h * (2 * PEER_NKEYS), 2 * PEER_NKEYS)
        s1 = s_ref[pl.ds(base, PEER_NKEYS), :]
        s2 = s_ref[pl.ds(base + PEER_NKEYS, PEER_NKEYS), :]
        (v1, rank1), (v2, rank2) = _topk_rows([s1, s2], k)
        pieces, pos = [], []
        for r1 in range(half):
            pieces.append(v1[r1:r1 + 1, :] + v2[0:half, :])
            pos.append(sub + r1 * k)
        pieces.append(v1[0:1, :] + v2[half:k, :])
        pos.append(sub + half)
        pieces.append(v1[half:k, :] + v2[0:1, :])
        pos.append((sub + half) * k)
        cand = jnp.concatenate(pieces, axis=0)
        posid = jnp.concatenate(pos, axis=0)
        big = k * k

        def pick_all(exact_ties):
            def pick(_, w):
                mx = jnp.max(w, axis=0, keepdims=True)
                hit = w == mx
                if exact_ties:
                    hit = posid == jnp.min(jnp.where(hit, posid, big), axis=0, keepdims=True)
                return jnp.where(hit, NEG_INF, w)

            return (lax.fori_loop(0, k, pick, cand) == NEG_INF).astype(F32)

        sel_fast = pick_all(False)
        picked = jnp.sum(sel_fast, axis=0, keepdims=True)
        sel = lax.cond(jnp.max(picked) == k, lambda: sel_fast, lambda: pick_all(True))
        top = v1[0:1, :] + v2[0:1, :]
        z = jnp.sum(sel * jnp.exp(cand - top), axis=0, keepdims=True)
        counts = []
        for r1 in range(half):
            c = jnp.sum(sel[r1 * half:(r1 + 1) * half, :], axis=0, keepdims=True)
            if r1 == 0:
                c = c + jnp.sum(sel[half * half:half * half + half, :], axis=0, keepdims=True)
            counts.append(c)
        tail = sel[half * half + half:, :]
        for r1 in range(half, k):
            counts.append(tail[r1 - half:r1 - half + 1, :])
        nsel = jnp.zeros(s1.shape, F32)
        for r1 in range(k):
            nsel = jnp.where(rank1 == r1, counts[r1], nsel)
        rank2_ref[h] = rank2.astype(F32)
        nsel_ref[h] = nsel
        e1_ref[h] = jnp.exp(s1 - v1[0:1, :])
        e2z_ref[h] = jnp.exp(s2 - v2[0:1, :]) / z
        return 0

    lax.fori_loop(0, PEER_HEADS, head, 0)


def peer_select(s_t):
    rows, t = s_t.shape
    spec = pl.BlockSpec((PEER_HEADS, PEER_NKEYS, LANES), lambda i: (0, 0, i))
    shp = jax.ShapeDtypeStruct((PEER_HEADS, PEER_NKEYS, t), F32)
    return pl.pallas_call(
        _peer_select_kernel,
        grid=(t // LANES,),
        in_specs=[pl.BlockSpec((rows, LANES), lambda i: (0, i))],
        out_specs=[spec] * 4,
        out_shape=[shp] * 4,
        compiler_params=_params("parallel"),
        name="peer_select",
    )(s_t)


def _gelu(x):
    return 0.5 * x * (1.0 + lax.erf(x * (1.0 / math.sqrt(2.0))))


def _peer_dense_kernel(xn_ref, u_ref, vt_ref, rank2_ref, e2z_ref, nsel_ref, e1_ref, o_ref, a_ref):
    j = pl.program_id(1)
    nb = u_ref.shape[0] // PEER_NKEYS

    @pl.when(j == 0)
    def _():
        o_ref[...] = jnp.zeros_like(o_ref)

    for kb in range(nb):
        a = j * nb + kb
        rows = slice(kb * PEER_NKEYS, (kb + 1) * PEER_NKEYS)
        s = jnp.dot(u_ref[rows, :], xn_ref[...], preferred_element_type=F32)
        g = jnp.zeros(s.shape, F32)
        for h in range(PEER_HEADS):
            n_row = nsel_ref[h, pl.ds(a, 1), :]
            e_row = e1_ref[h, pl.ds(a, 1), :]
            g = g + jnp.where(rank2_ref[h] < n_row, e2z_ref[h], 0.0) * e_row
        a_ref[rows, :] = (_gelu(s) * g).astype(BF16)
    o_ref[...] += jnp.dot(vt_ref[...], a_ref[...], preferred_element_type=F32)


def peer_dense(xn_t, u_bf, vt_bf, rank2, e2z, nsel, e1, tm=512, te=512):
    d, t = xn_t.shape
    n_exp = u_bf.shape[0]
    sel_spec = pl.BlockSpec((PEER_HEADS, PEER_NKEYS, tm), lambda i, j: (0, 0, i))
    return pl.pallas_call(
        _peer_dense_kernel,
        grid=(t // tm, n_exp // te),
        in_specs=[pl.BlockSpec((d, tm), lambda i, j: (0, i)),
                  pl.BlockSpec((te, d), lambda i, j: (j, 0)),
                  pl.BlockSpec((d, te), lambda i, j: (0, j)),
                  sel_spec, sel_spec, sel_spec, sel_spec],
        out_specs=pl.BlockSpec((d, tm), lambda i, j: (0, i)),
        out_shape=jax.ShapeDtypeStruct((d, t), F32),
        scratch_shapes=[pltpu.VMEM((te, tm), BF16)],
        compiler_params=_params("parallel", "arbitrary"),
        name="peer_dense",
    )(xn_t, u_bf, vt_bf, rank2, e2z, nsel, e1)


def peer(x, g, w_q, sub_keys, u_tab, v_tab):
    t, d = x.shape
    xn_t = rmsnorm_bf16_t(x, g)
    wq_t = w_q.T.astype(BF16)
    q_t = mm(wq_t, xn_t, _pick(wq_t.shape[0], (512, 256, 128)), _pick(t, (512, 256, 128)), BF16)
    hc = PEER_HEADS * 2
    dk2 = PEER_DK // 2
    keys = sub_keys.reshape(hc, PEER_NKEYS, dk2).astype(BF16)
    eye = jnp.eye(hc, dtype=BF16)
    kbd_t = (keys[:, :, None, :] * eye[:, None, :, None]).reshape(hc * PEER_NKEYS, hc * dk2)
    s_t = mm(kbd_t, q_t, _pick(kbd_t.shape[0], (512, 256, 128)), _pick(t, (512, 256, 128)))
    rank2, e2z, nsel, e1 = peer_select(s_t)
    out_t = peer_dense(xn_t, u_tab.astype(BF16), cast_transposed_bf16(v_tab), rank2, e2z, nsel, e1,
                       tm=_pick(t, (512, 256, 128)))
    return add_transposed(x, out_t)


def _fox_norm_kernel(p_ref, gq_ref, gk_ref, q_ref, k_ref):
    scale = FOX_DH ** -0.5
    for h in range(FOX_HEADS):
        lo, hi = h * FOX_DH, (h + 1) * FOX_DH
        x = p_ref[:, lo:hi]
        y = x * lax.rsqrt(jnp.mean(x * x, axis=-1, keepdims=True) + RMS_EPS) * gq_ref[...]
        q_ref[:, lo:hi] = (y * scale).astype(q_ref.dtype)
        x = p_ref[:, FOX_W + lo:FOX_W + hi]
        k_ref[:, lo:hi] = x * lax.rsqrt(jnp.mean(x * x, axis=-1, keepdims=True) + RMS_EPS) * gk_ref[...]


def fox_norm(p, g_q, g_k):
    t = p.shape[0]
    tm = _pick(t, (256, 128))
    return pl.pallas_call(
        _fox_norm_kernel,
        grid=(t // tm,),
        in_specs=[pl.BlockSpec((tm, 2 * FOX_W), lambda i: (i, 0)),
                  pl.BlockSpec((1, FOX_DH), lambda i: (0, 0)), pl.BlockSpec((1, FOX_DH), lambda i: (0, 0))],
        out_specs=[pl.BlockSpec((tm, FOX_W), lambda i: (i, 0))] * 2,
        out_shape=[jax.ShapeDtypeStruct((t, FOX_W), BF16), jax.ShapeDtypeStruct((t, FOX_W), F32)],
        compiler_params=_params("parallel"),
        name="fox_norm",
    )(p, g_q.reshape(1, FOX_DH), g_k.reshape(1, FOX_DH))


def _fox_prompt_kernel(qi_ref, ki_ref, q_ref, k_ref, v_ref, f_ref, o_ref, m_ref, l_ref, acc_ref, *, tq, tk):
    n = pl.program_id(2)
    i = qi_ref[n]
    kk = ki_ref[n]

    @pl.when(kk == 0)
    def _():
        m_ref[...] = jnp.full_like(m_ref, NEG_INF)
        l_ref[...] = jnp.zeros_like(l_ref)
        acc_ref[...] = jnp.zeros_like(acc_ref)

    def step(masked):
        s = lax.dot_general(q_ref[...], k_ref[...].astype(BF16), (((1,), (1,)), ((), ())),
                            preferred_element_type=F32)
        s = s - f_ref[0]
        if masked:
            q_pos = i * tq + lax.broadcasted_iota(jnp.int32, s.shape, 0)
            k_pos = kk * tk + lax.broadcasted_iota(jnp.int32, s.shape, 1)
            s = jnp.where(q_pos >= k_pos, s, NEG_INF)
        m_prev = m_ref[...]
        m_new = jnp.maximum(m_prev, jnp.max(s, axis=1, keepdims=True))
        alpha = jnp.exp(m_prev - m_new)
        p = jnp.exp(s - m_new)
        l_ref[...] = alpha * l_ref[...] + jnp.sum(p, axis=1, keepdims=True)
        acc_ref[...] = alpha * acc_ref[...] + jnp.dot(p.astype(BF16), v_ref[...].astype(BF16),
                                                      preferred_element_type=F32)
        m_ref[...] = m_new

    on_diag = (kk + 1) * tk > i * tq + 1

    @pl.when(on_diag)
    def _():
        step(True)

    @pl.when(jnp.logical_not(on_diag))
    def _():
        step(False)

    @pl.when(kk == (i * tq + tq - 1) // tk)
    def _():
        o_ref[...] = (acc_ref[...] / l_ref[...]).astype(o_ref.dtype)


def fox_prompt(q_bf, k, p, f_cum, bsz, s_len, tq=1024, tk=1024):
    nq, nk = s_len // tq, s_len // tk
    pairs = [(i, kk) for i in range(nq) for kk in range((i * tq + tq - 1) // tk + 1)]
    qi = jnp.asarray([a for a, _ in pairs], jnp.int32)
    ki = jnp.asarray([b for _, b in pairs], jnp.int32)
    v_col0 = 2 * FOX_W // FOX_DH
    grid_spec = pltpu.PrefetchScalarGridSpec(
        num_scalar_prefetch=2,
        grid=(bsz, FOX_HEADS, len(pairs)),
        in_specs=[pl.BlockSpec((tq, FOX_DH), lambda b, h, n, qi, ki: (b * nq + qi[n], h)),
                  pl.BlockSpec((tk, FOX_DH), lambda b, h, n, qi, ki: (b * nk + ki[n], h)),
                  pl.BlockSpec((tk, FOX_DH), lambda b, h, n, qi, ki: (b * nk + ki[n], v_col0 + h)),
                  pl.BlockSpec((1, 1, tk), lambda b, h, n, qi, ki: (b * FOX_HEADS + h, 0, ki[n]))],
        out_specs=pl.BlockSpec((tq, FOX_DH), lambda b, h, n, qi, ki: (b * nq + qi[n], h)),
        scratch_shapes=[pltpu.VMEM((tq, 1), F32), pltpu.VMEM((tq, 1), F32), pltpu.VMEM((tq, FOX_DH), F32)],
    )
    return pl.pallas_call(
        functools.partial(_fox_prompt_kernel, tq=tq, tk=tk),
        grid_spec=grid_spec,
        out_shape=jax.ShapeDtypeStruct((bsz * s_len, FOX_W), BF16),
        compiler_params=_params("parallel", "parallel", "arbitrary"),
        name="fox_prompt",
    )(qi, ki, q_bf, k, p, f_cum)


SUB = 8


def _logf_cumsum_kernel(pt_ref, *refs):
    lf_refs, o_ref = refs[:-1], refs[-1]
    page, nh = lf_refs[0].shape[1:]
    tri = (lax.broadcasted_iota(jnp.int32, (page, page), 1)
           <= lax.broadcasted_iota(jnp.int32, (page, page), 0)).astype(F32)
    carry = jnp.zeros((1, nh), F32)
    for p, lf_ref in enumerate(lf_refs):
        f = jnp.dot(tri, lf_ref[0], precision=lax.Precision.HIGHEST, preferred_element_type=F32) + carry
        carry = f[page - 1:page, :]
        o_ref[0, p] = f


def paged_logf_cumsum(cache_logf, page_table):
    bsz, n_pages = page_table.shape
    _, page, nh = cache_logf.shape

    def page_spec(p):
        return pl.BlockSpec((1, page, nh), lambda b, pt: (pt[b, p], 0, 0))

    grid_spec = pltpu.PrefetchScalarGridSpec(
        num_scalar_prefetch=1,
        grid=(bsz,),
        in_specs=[page_spec(p) for p in range(n_pages)],
        out_specs=pl.BlockSpec((1, n_pages, page, nh), lambda b, pt: (b, 0, 0, 0)),
    )
    return pl.pallas_call(
        _logf_cumsum_kernel,
        grid_spec=grid_spec,
        out_shape=jax.ShapeDtypeStruct((bsz, n_pages, page, nh), F32),
        compiler_params=_params("parallel"),
        name="logf_cumsum",
    )(page_table, *([cache_logf] * n_pages))


def _fox_sample_kernel(pt_ref, q_ref, *refs, n_groups, group):
    kc_refs, vc_refs = refs[:group], refs[group:2 * group]
    f_ref, mask_ref, kn_ref, vn_ref, fn_ref, maskn_ref, o_ref, m_ref, l_ref, acc_ref = refs[2 * group:]
    pg = pl.program_id(1)

    @pl.when(pg == 0)
    def _():
        m_ref[...] = jnp.full_like(m_ref, NEG_INF)
        l_ref[...] = jnp.zeros_like(l_ref)
        acc_ref[...] = jnp.zeros_like(acc_ref)

    def attend(kvb):
        ss = [lax.dot_general(q_ref[0], k2, (((1,), (1,)), ((), ())), preferred_element_type=F32) + b
              for k2, _, b in kvb]
        m_prev = m_ref[...]
        m_new = m_prev
        for s in ss:
            m_new = jnp.maximum(m_new, jnp.max(s, axis=1, keepdims=True))
        alpha = jnp.exp(m_prev - m_new)
        l_new = alpha * l_ref[...]
        acc = alpha * acc_ref[...]
        for s, (_, v2, _) in zip(ss, kvb):
            p = jnp.exp(s - m_new)
            l_new = l_new + jnp.sum(p, axis=1, keepdims=True)
            acc = acc + jnp.dot(p.astype(BF16), v2, preferred_element_type=F32)
        l_ref[...] = l_new
        acc_ref[...] = acc
        m_ref[...] = m_new

    @pl.when(pg < n_groups)
    def _():
        rows = kc_refs[0].shape[0] * kc_refs[0].shape[1]
        attend([(kc[...].reshape(rows, FOX_DH).astype(BF16), vc[...].reshape(rows, FOX_DH).astype(BF16),
                 mask_ref[...] - f_ref[0, g])
                for g, (kc, vc) in enumerate(zip(kc_refs, vc_refs))])

    @pl.when(pg == n_groups)
    def _():
        attend([(kn_ref[0].astype(BF16), vn_ref[0].astype(BF16), maskn_ref[...] - fn_ref[0])])
        o_ref[0] = acc_ref[...] / l_ref[...]


def fox_sample(q_bf, k_new, v_new, lf_new, cache_k, cache_v, cache_logf, page_table):
    bsz, t_len, _ = q_bf.shape
    n_pool, page = cache_k.shape[:2]
    n_pages = page_table.shape[1]
    nh, dh = FOX_HEADS, FOX_DH
    rows = nh * t_len
    lanes = page * nh
    lanes_new = SUB * nh
    qf = q_bf.reshape(bsz, t_len, nh, dh).transpose(0, 2, 1, 3).reshape(bsz, rows, dh)
    f_pages = paged_logf_cumsum(cache_logf, page_table)
    f_rows = f_pages.reshape(bsz, n_pages, 1, lanes)
    pad = ((0, 0), (0, SUB - t_len), (0, 0))
    kn = jnp.pad(k_new, pad).reshape(bsz, lanes_new, dh)
    vn = jnp.pad(v_new, pad).reshape(bsz, lanes_new, dh)
    fn = (f_pages[:, -1, -1, :][:, None, :] + jnp.cumsum(jnp.pad(lf_new, pad), axis=1)).reshape(bsz, 1, lanes_new)
    row_head, row_q = np.arange(rows) // t_len, np.arange(rows) % t_len
    mask = np.where(row_head[:, None] == (np.arange(lanes) % nh)[None, :], 0.0, NEG_INF).astype(np.float32)
    key_n, head_n = np.arange(lanes_new) // nh, np.arange(lanes_new) % nh
    ok = (row_head[:, None] == head_n[None, :]) & (key_n[None, :] <= row_q[:, None]) & (key_n[None, :] < t_len)
    mask_new = np.where(ok, 0.0, NEG_INF).astype(np.float32)
    group = _pick(n_pages, (4, 2, 1))
    n_groups = n_pages // group
    last = n_groups - 1

    def cache_spec(g):
        return pl.BlockSpec((page, nh, dh), lambda b, p, pt: (pt[b, jnp.minimum(p, last) * group + g], 0, 0))

    cache_specs = [cache_spec(g) for g in range(group)]
    kc3 = cache_k.reshape(n_pool * page, nh, dh)
    vc3 = cache_v.reshape(n_pool * page, nh, dh)
    grid_spec = pltpu.PrefetchScalarGridSpec(
        num_scalar_prefetch=1,
        grid=(bsz, n_groups + 1),
        in_specs=[pl.BlockSpec((1, rows, dh), lambda b, p, pt: (b, 0, 0)),
                  *cache_specs, *cache_specs,
                  pl.BlockSpec((1, group, 1, lanes), lambda b, p, pt: (b, jnp.minimum(p, last), 0, 0)),
                  pl.BlockSpec((rows, lanes), lambda b, p, pt: (0, 0)),
                  pl.BlockSpec((1, lanes_new, dh), lambda b, p, pt: (b, 0, 0)),
                  pl.BlockSpec((1, lanes_new, dh), lambda b, p, pt: (b, 0, 0)),
                  pl.BlockSpec((1, 1, lanes_new), lambda b, p, pt: (b, 0, 0)),
                  pl.BlockSpec((rows, lanes_new), lambda b, p, pt: (0, 0))],
        out_specs=pl.BlockSpec((1, rows, dh), lambda b, p, pt: (b, 0, 0)),
        scratch_shapes=[pltpu.VMEM((rows, 1), F32), pltpu.VMEM((rows, 1), F32), pltpu.VMEM((rows, dh), F32)],
    )
    out = pl.pallas_call(
        functools.partial(_fox_sample_kernel, n_groups=n_groups, group=group),
        grid_spec=grid_spec,
        out_shape=jax.ShapeDtypeStruct((bsz, rows, dh), F32),
        compiler_params=_params("parallel", "arbitrary"),
        name="fox_sample",
    )(page_table, qf, *([kc3] * group), *([vc3] * group),
      f_rows, jnp.asarray(mask), kn, vn, fn, jnp.asarray(mask_new))
    return out.reshape(bsz, nh, t_len, dh).transpose(0, 2, 1, 3).reshape(bsz, t_len, nh * dh)


def _mlstm_kernel(q_ref, k_ref, v_ref, o_ref, g_ref, c0_ref, n0_ref, m0_ref, mg_ref,
                  h_ref, c1_ref, n1_ref, m1_ref, c_scr, n_scr, m_scr):
    c = pl.program_id(1)
    L = q_ref.shape[0]
    nh, dh = MLSTM_HEADS, MLSTM_DH

    @pl.when(c == 0)
    def _():
        c_scr[...] = c0_ref[0]
        n_scr[...] = n0_ref[0]
        m_scr[...] = m0_ref[0]

    t_idx = lax.broadcasted_iota(jnp.int32, (L, L), 0)
    s_idx = lax.broadcasted_iota(jnp.int32, (L, L), 1)
    causal = s_idx <= t_idx
    diag = s_idx == t_idx
    for h in range(nh):
        cols = slice(h * dh, (h + 1) * dh)
        ig = g_ref[0, h:h + 1, :]
        lf = g_ref[0, nh + h:nh + h + 1, :]
        m0 = m_scr[h:h + 1, :]
        f_col = jnp.sum(jnp.where(causal, lf, 0.0), axis=1, keepdims=True)
        ig_col = jnp.sum(jnp.where(diag, ig, 0.0), axis=1, keepdims=True)
        g_col = ig_col - f_col
        g_row = jnp.sum(jnp.where(diag, g_col, 0.0), axis=0, keepdims=True)
        cm_col = jnp.max(jnp.where(causal, g_row, NEG_INF), axis=1, keepdims=True)
        m_col = f_col + jnp.maximum(m0, cm_col)
        b_col = f_col - m_col
        a_col = jnp.exp(f_col + m0 - m_col)
        d = jnp.exp(jnp.where(causal, b_col + g_row, NEG_INF))
        q = q_ref[:, cols] * (dh ** -0.5)
        k = k_ref[:, cols]
        v = v_ref[:, cols]
        qb, kb, vb = q.astype(BF16), k.astype(BF16), v.astype(BF16)
        qk = lax.dot_general(qb, kb, (((1,), (1,)), ((), ())), preferred_element_type=F32) * d
        c_old = c_scr[h]
        n_old = n_scr[h:h + 1, :]
        num = a_col * jnp.dot(qb, c_old.astype(BF16), preferred_element_type=F32) \
            + jnp.dot(qk.astype(BF16), vb, preferred_element_type=F32)
        den = a_col * jnp.sum(q * n_old, axis=1, keepdims=True) + jnp.sum(qk, axis=1, keepdims=True)
        hs = num / jnp.maximum(jnp.abs(den), jnp.exp(-m_col))
        hn = hs * lax.rsqrt(jnp.mean(hs * hs, axis=1, keepdims=True) + RMS_EPS) * mg_ref[:, cols]
        h_ref[:, cols] = (hn * jax.nn.sigmoid(o_ref[:, cols])).astype(h_ref.dtype)
        b_last = b_col[L - 1:L, :]
        a_last = a_col[L - 1:L, :]
        kd = k * jnp.exp(b_last + g_col)
        c_scr[h] = a_last * c_old + lax.dot_general(kd.astype(BF16), vb, (((0,), (0,)), ((), ())),
                                                     preferred_element_type=F32)
        n_scr[h:h + 1, :] = a_last * n_old + jnp.sum(kd, axis=0, keepdims=True)
        m_scr[h:h + 1, :] = m_col[L - 1:L, :]

    @pl.when(c == pl.num_programs(1) - 1)
    def _():
        c1_ref[0] = c_scr[...]
        n1_ref[0] = n_scr[...]
        m1_ref[0] = m_scr[...]


def mlstm(p, gates, c0, n0, m0, mh_g, bsz, n_chunks, chunk, col0):
    nh, dh, mw = MLSTM_HEADS, MLSTM_DH, MLSTM_W
    rows = bsz * n_chunks * chunk

    def col_spec(j):
        return pl.BlockSpec((chunk, mw), lambda b, c: (b * n_chunks + c, col0 + j))

    state = lambda b, c: (b, 0, 0)
    return pl.pallas_call(
        _mlstm_kernel,
        grid=(bsz, n_chunks),
        in_specs=[col_spec(0), col_spec(1), col_spec(2), col_spec(3),
                  pl.BlockSpec((1, 2 * nh, chunk), lambda b, c: (b * n_chunks + c, 0, 0)),
                  pl.BlockSpec((1, nh, dh, dh), lambda b, c: (b, 0, 0, 0)),
                  pl.BlockSpec((1, nh, dh), state), pl.BlockSpec((1, nh, 1), state),
                  pl.BlockSpec((1, mw), lambda b, c: (0, 0))],
        out_specs=[pl.BlockSpec((chunk, mw), lambda b, c: (b * n_chunks + c, 0)),
                   pl.BlockSpec((1, nh, dh, dh), lambda b, c: (b, 0, 0, 0)),
                   pl.BlockSpec((1, nh, dh), state), pl.BlockSpec((1, nh, 1), state)],
        out_shape=[jax.ShapeDtypeStruct((rows, mw), BF16),
                   jax.ShapeDtypeStruct((bsz, nh, dh, dh), F32),
                   jax.ShapeDtypeStruct((bsz, nh, dh), F32),
                   jax.ShapeDtypeStruct((bsz, nh, 1), F32)],
        scratch_shapes=[pltpu.VMEM((nh, dh, dh), F32), pltpu.VMEM((nh, dh), F32), pltpu.VMEM((nh, 1), F32)],
        compiler_params=_params("parallel", "arbitrary"),
        name="mlstm",
    )(p, p, p, p, gates, c0, n0, m0.reshape(bsz, nh, 1), mh_g.reshape(1, mw))


CONV_HALO = 32
CONV_ROWS = 32


def _group_ln_swish(y, g, b):
    outs = []
    for grp in range(CONV_GROUPS):
        x = y[:, grp * LANES:(grp + 1) * LANES]
        mu = jnp.mean(x, axis=1, keepdims=True)
        xc = x - mu
        var = jnp.mean(xc * xc, axis=1, keepdims=True)
        z = xc * lax.rsqrt(var + LN_EPS) * g[:, grp * LANES:(grp + 1) * LANES] + b[:, grp * LANES:(grp + 1) * LANES]
        outs.append(z * jax.nn.sigmoid(z))
    return jnp.concatenate(outs, axis=1)


def _conv_prompt_kernel(cv_ref, cg_ref, hv_ref, hg_ref, w_ref, b_ref, g_ref, beta_ref, y_ref, buf_ref, ext_ref):
    i = pl.program_id(1)
    ts = cv_ref.shape[0]
    hist = jnp.where(i > 0, hv_ref[...] * jax.nn.sigmoid(hg_ref[...]), 0.0)
    ext_ref[0:CONV_HALO, :] = hist
    ext_ref[CONV_HALO:, :] = cv_ref[...] * jax.nn.sigmoid(cg_ref[...])
    off = CONV_HALO - (CONV_K - 1)
    for r0 in range(0, ts, CONV_ROWS):
        acc = jnp.zeros((CONV_ROWS, CONV_W), F32) + b_ref[...]
        for j in range(CONV_K):
            acc = acc + ext_ref[r0 + off + j:r0 + off + j + CONV_ROWS, :] * w_ref[j:j + 1, :]
        y_ref[r0:r0 + CONV_ROWS, :] = _group_ln_swish(acc, g_ref[...], beta_ref[...]).astype(y_ref.dtype)

    @pl.when(i == pl.num_programs(1) - 1)
    def _():
        buf_ref[0] = ext_ref[ts + off:ts + CONV_HALO, :]


def conv_prompt(p, conv_w, conv_b, cn_g, cn_b, bsz, s_len, ts=512):
    nt = s_len // ts
    hb = ts // CONV_HALO

    def halo(col):
        return pl.BlockSpec((CONV_HALO, CONV_W), lambda b, i: (jnp.maximum((b * nt + i) * hb - 1, 0), col))

    vec = pl.BlockSpec((1, CONV_W), lambda b, i: (0, 0))
    return pl.pallas_call(
        _conv_prompt_kernel,
        grid=(bsz, nt),
        in_specs=[pl.BlockSpec((ts, CONV_W), lambda b, i: (b * nt + i, 0)),
                  pl.BlockSpec((ts, CONV_W), lambda b, i: (b * nt + i, 1)),
                  halo(0), halo(1),
                  pl.BlockSpec((CONV_K, CONV_W), lambda b, i: (0, 0)), vec, vec, vec],
        out_specs=[pl.BlockSpec((ts, CONV_W), lambda b, i: (b * nt + i, 0)),
                   pl.BlockSpec((1, CONV_K - 1, CONV_W), lambda b, i: (b, 0, 0))],
        out_shape=[jax.ShapeDtypeStruct((bsz * s_len, CONV_W), BF16),
                   jax.ShapeDtypeStruct((bsz, CONV_K - 1, CONV_W), F32)],
        scratch_shapes=[pltpu.VMEM((ts + CONV_HALO, CONV_W), F32)],
        compiler_params=_params("parallel", "arbitrary"),
        name="conv_prompt",
    )(p, p, p, p, conv_w, conv_b.reshape(1, -1), cn_g.reshape(1, -1), cn_b.reshape(1, -1))


def _conv_sample_kernel(cv_ref, cg_ref, st_ref, w_ref, b_ref, g_ref, beta_ref, y_ref, buf_ref, ext_ref, *, t_len):
    nb = st_ref.shape[0]
    hist = CONV_K - 1
    y_ref[...] = jnp.zeros_like(y_ref)
    for bb in range(nb):
        rows = slice(bb * SUB, bb * SUB + t_len)
        ext_ref[0:hist, :] = st_ref[bb]
        ext_ref[hist:hist + t_len, :] = cv_ref[rows, :] * jax.nn.sigmoid(cg_ref[rows, :])
        ys = [jnp.sum(ext_ref[t:t + CONV_K, :] * w_ref[...], axis=0, keepdims=True) for t in range(t_len)]
        y = jnp.concatenate(ys, axis=0) + b_ref[...]
        y_ref[rows, :] = _group_ln_swish(y, g_ref[...], beta_ref[...]).astype(y_ref.dtype)
        buf_ref[bb] = ext_ref[t_len:t_len + hist, :]


def conv_sample(p_pad, state_conv, conv_w, conv_b, cn_g, cn_b, t_len, nb=8):
    bsz = state_conv.shape[0]
    vec = pl.BlockSpec((1, CONV_W), lambda i: (0, 0))
    return pl.pallas_call(
        functools.partial(_conv_sample_kernel, t_len=t_len),
        grid=(bsz // nb,),
        in_specs=[pl.BlockSpec((nb * SUB, CONV_W), lambda i: (i, 0)),
                  pl.BlockSpec((nb * SUB, CONV_W), lambda i: (i, 1)),
                  pl.BlockSpec((nb, CONV_K - 1, CONV_W), lambda i: (i, 0, 0)),
                  pl.BlockSpec((CONV_K, CONV_W), lambda i: (0, 0)), vec, vec, vec],
        out_specs=[pl.BlockSpec((nb * SUB, CONV_W), lambda i: (i, 0)),
                   pl.BlockSpec((nb, CONV_K - 1, CONV_W), lambda i: (i, 0, 0))],
        out_shape=[jax.ShapeDtypeStruct((bsz * SUB, CONV_W), BF16),
                   jax.ShapeDtypeStruct((bsz, CONV_K - 1, CONV_W), F32)],
        scratch_shapes=[pltpu.VMEM((CONV_K - 1 + SUB, CONV_W), F32)],
        compiler_params=_params("parallel"),
        name="conv_sample",
    )(p_pad, p_pad, state_conv, conv_w, conv_b.reshape(1, -1), cn_g.reshape(1, -1), cn_b.reshape(1, -1))


def kernel(x_prompt, x_sample, state_conv, state_C, state_n, state_m, cache_k, cache_v, cache_logf, page_table,
           norm0_mix, w_in0, b_ig0, b_fg0, conv_w0, conv_b0, conv_norm_g0, conv_norm_b0, mlstm_norm_g0, w_out0,
           norm1_mix, w_in1, b_f1, q_norm_g1, k_norm_g1, w_out1,
           norm0_ffn, peer_wq0, peer_keys0, peer_u0, peer_v0,
           norm1_ffn, peer_wq1, peer_keys1, peer_u1, peer_v1):
    bp, sp, d = x_prompt.shape
    bs, ss, _ = x_sample.shape
    tp, ts = bp * sp, bs * ss
    x = jnp.concatenate([x_prompt.reshape(tp, d), x_sample.reshape(ts, d)], axis=0)

    n_main0 = 2 * CONV_W + 4 * MLSTM_W
    xn = rmsnorm_bf16(x, norm0_mix)
    p0 = mm_tokens(xn, w_in0[:, :n_main0].astype(BF16))
    g0 = gate_proj(x, norm0_mix, w_in0[:, n_main0:])
    nh = MLSTM_HEADS
    gates = jnp.concatenate([g0[:, :nh] + b_ig0, jax.nn.log_sigmoid(g0[:, nh:] + b_fg0)], axis=1)
    conv_par = (conv_w0, conv_b0, conv_norm_g0, conv_norm_b0)
    mcol0 = 2 * CONV_W // MLSTM_W
    nc = sp // MLSTM_CHUNK
    gates_p = gates[:tp].reshape(bp * nc, MLSTM_CHUNK, 2 * nh).transpose(0, 2, 1)
    hp, p_C, p_n, p_m = mlstm(p0, gates_p, jnp.zeros((bp, nh, MLSTM_DH, MLSTM_DH), F32),
                              jnp.zeros((bp, nh, MLSTM_DH), F32), jnp.zeros((bp, nh), F32),
                              mlstm_norm_g0, bp, nc, MLSTM_CHUNK, mcol0)
    yp, p_conv = conv_prompt(p0, *conv_par, bp, sp)
    tpad = ((0, 0), (0, SUB - ss), (0, 0))
    p0s = jnp.pad(p0[tp:].reshape(bs, ss, -1), tpad).reshape(bs * SUB, -1)
    gs = gates[tp:].reshape(bs, ss, 2 * nh)
    gates_s = jnp.concatenate([jnp.pad(gs[..., :nh], tpad, constant_values=NEG_INF),
                               jnp.pad(gs[..., nh:], tpad)], axis=-1).transpose(0, 2, 1)
    hs, s_C, s_n, s_m = mlstm(p0s, gates_s, state_C, state_n, state_m, mlstm_norm_g0, bs, 1, SUB, mcol0)
    ys, s_conv = conv_sample(p0s, state_conv, *conv_par, ss)
    unpad = lambda a: a.reshape(bs, SUB, -1)[:, :ss].reshape(ts, -1)
    y_all = jnp.concatenate([yp, unpad(ys)], axis=0)
    h_all = jnp.concatenate([hp[:tp], unpad(hs)], axis=0)
    w_out0_bf = w_out0.astype(BF16)
    tm_out = _pick(tp + ts, (512, 256, 128))
    x = mm_residual(x, [y_all, h_all], [w_out0_bf[:CONV_W], w_out0_bf[CONV_W:]], tm_out, 1024)
    p_m, s_m = p_m.reshape(bp, nh), s_m.reshape(bs, nh)
    x = peer(x, norm0_ffn, peer_wq0, peer_keys0, peer_u0, peer_v0)

    xn = rmsnorm_bf16(x, norm1_mix)
    p1 = mm_tokens(xn, w_in1[:, :3 * FOX_W].astype(BF16))
    g1 = gate_proj(x, norm1_mix, w_in1[:, 3 * FOX_W:])
    lf = jax.nn.log_sigmoid(g1 + b_f1)
    q_bf, k = fox_norm(p1, q_norm_g1, k_norm_g1)
    v = p1[:, 2 * FOX_W:]
    f_cum = jnp.cumsum(lf[:tp].reshape(bp, sp, FOX_HEADS), axis=1).transpose(0, 2, 1).reshape(bp * FOX_HEADS, 1, sp)
    attn_p = fox_prompt(q_bf, k, p1, f_cum, bp, sp)
    attn_s = fox_sample(q_bf[tp:].reshape(bs, ss, FOX_W), k[tp:].reshape(bs, ss, FOX_W),
                        v[tp:].reshape(bs, ss, FOX_W), lf[tp:].reshape(bs, ss, FOX_HEADS),
                        cache_k, cache_v, cache_logf, page_table)
    attn = jnp.concatenate([attn_p, attn_s.reshape(ts, FOX_W).astype(BF16)], axis=0)
    x = mm_residual(x, [attn], [w_out1.astype(BF16)], tm_out, 1024)
    x = peer(x, norm1_ffn, peer_wq1, peer_keys1, peer_u1, peer_v1)

    hd = (FOX_HEADS, FOX_DH)
    return (x[:tp].reshape(bp, sp, d), x[tp:].reshape(bs, ss, d), p_conv, p_C, p_n, p_m,
            k[:tp].reshape(bp, sp, *hd), v[:tp].reshape(bp, sp, *hd), lf[:tp].reshape(bp, sp, FOX_HEADS),
            s_conv, s_C, s_n, s_m,
            k[tp:].reshape(bs, ss, *hd), v[tp:].reshape(bs, ss, *hd), lf[tp:].reshape(bs, ss, FOX_HEADS))
```

```python
import functools
import math

import jax
import jax.numpy as jnp
import numpy as np
from jax import lax
from jax.experimental import pallas as pl
from jax.experimental.pallas import tpu as pltpu

F32 = jnp.float32
BF16 = jnp.bfloat16

D_MODEL = 2048
CONV_W = 1024
CONV_K = 31
CONV_GROUPS = 8
MLSTM_HEADS = 4
MLSTM_W = 1024
MLSTM_DH = 256
MLSTM_CHUNK = 128
FOX_HEADS = 16
FOX_DH = 128
FOX_W = 2048
PEER_HEADS = 8
PEER_NKEYS = 128
PEER_DK = 128
PEER_TOPK = 16
RMS_EPS = 1e-6
LN_EPS = 1e-5

LANES = 128
VMEM_LIMIT = 56 * 1024 * 1024
NEG_INF = float("-inf")


def _params(*sem):
    return pltpu.CompilerParams(dimension_semantics=sem, vmem_limit_bytes=VMEM_LIMIT)


def _rmsnorm_kernel(x_ref, g_ref, o_ref):
    x = x_ref[...]
    y = x * lax.rsqrt(jnp.mean(x * x, axis=-1, keepdims=True) + RMS_EPS)
    o_ref[...] = (y * g_ref[...]).astype(o_ref.dtype)


def rmsnorm_bf16(x, g):
    t, d = x.shape
    tm = _pick(t, (512, 256, 128))
    return pl.pallas_call(
        _rmsnorm_kernel,
        grid=(t // tm,),
        in_specs=[pl.BlockSpec((tm, d), lambda i: (i, 0)), pl.BlockSpec((1, d), lambda i: (0, 0))],
        out_specs=pl.BlockSpec((tm, d), lambda i: (i, 0)),
        out_shape=jax.ShapeDtypeStruct((t, d), BF16),
        compiler_params=_params("parallel"),
        name="rmsnorm",
    )(x, g.reshape(1, d))


def _rmsnorm_t_kernel(x_ref, g_ref, o_ref):
    x = x_ref[...]
    y = x * lax.rsqrt(jnp.mean(x * x, axis=-1, keepdims=True) + RMS_EPS) * g_ref[...]
    o_ref[...] = y.T.astype(o_ref.dtype)


def rmsnorm_bf16_t(x, g):
    t, d = x.shape
    tm = _pick(t, (512, 256, 128))
    return pl.pallas_call(
        _rmsnorm_t_kernel,
        grid=(t // tm,),
        in_specs=[pl.BlockSpec((tm, d), lambda i: (i, 0)), pl.BlockSpec((1, d), lambda i: (0, 0))],
        out_specs=pl.BlockSpec((d, tm), lambda i: (0, i)),
        out_shape=jax.ShapeDtypeStruct((d, t), BF16),
        compiler_params=_params("parallel"),
        name="rmsnorm_t",
    )(x, g.reshape(1, d))


def _cast_t_kernel(x_ref, o_ref):
    o_ref[...] = x_ref[...].T.astype(o_ref.dtype)


def cast_transposed_bf16(w):
    n, d = w.shape
    tn = _pick(n, (512, 256, 128))
    return pl.pallas_call(
        _cast_t_kernel,
        grid=(n // tn,),
        in_specs=[pl.BlockSpec((tn, d), lambda i: (i, 0))],
        out_specs=pl.BlockSpec((d, tn), lambda i: (0, i)),
        out_shape=jax.ShapeDtypeStruct((d, n), BF16),
        compiler_params=_params("parallel"),
        name="cast_t",
    )(w)


def _add_t_kernel(x_ref, yt_ref, o_ref):
    o_ref[...] = x_ref[...] + yt_ref[...].T


def add_transposed(x, y_t):
    t, d = x.shape
    tm = _pick(t, (512, 256, 128))
    return pl.pallas_call(
        _add_t_kernel,
        grid=(t // tm,),
        in_specs=[pl.BlockSpec((tm, d), lambda i: (i, 0)), pl.BlockSpec((d, tm), lambda i: (0, i))],
        out_specs=pl.BlockSpec((tm, d), lambda i: (i, 0)),
        out_shape=jax.ShapeDtypeStruct((t, d), F32),
        compiler_params=_params("parallel"),
        name="add_t",
    )(x, y_t)


def _gate_proj_kernel(x_ref, g_ref, w_ref, o_ref):
    x = x_ref[...]
    y = x * lax.rsqrt(jnp.mean(x * x, axis=-1, keepdims=True) + RMS_EPS) * g_ref[...]
    o_ref[...] = jnp.dot(y, w_ref[...], precision=lax.Precision.HIGHEST, preferred_element_type=F32)


def gate_proj(x, g, w):
    t, d = x.shape
    n = w.shape[1]
    tm = _pick(t, (256, 128))
    out = pl.pallas_call(
        _gate_proj_kernel,
        grid=(t // tm,),
        in_specs=[pl.BlockSpec((tm, d), lambda i: (i, 0)), pl.BlockSpec((1, d), lambda i: (0, 0)),
                  pl.BlockSpec((d, LANES), lambda i: (0, 0))],
        out_specs=pl.BlockSpec((tm, LANES), lambda i: (i, 0)),
        out_shape=jax.ShapeDtypeStruct((t, LANES), F32),
        compiler_params=_params("parallel"),
        name="gate_proj",
    )(x, g.reshape(1, d), jnp.pad(w, ((0, 0), (0, LANES - n))))
    return out[:, :n]


def _mm_kernel(x_ref, w_ref, o_ref):
    o_ref[...] = jnp.dot(x_ref[...], w_ref[...], preferred_element_type=F32).astype(o_ref.dtype)


def mm(x, w, tm, tn, out_dtype=F32):
    m, k = x.shape
    n = w.shape[1]
    assert m % tm == 0 and n % tn == 0, (m, n, tm, tn)
    return pl.pallas_call(
        _mm_kernel,
        grid=(n // tn, m // tm),
        in_specs=[pl.BlockSpec((tm, k), lambda j, i: (i, 0)), pl.BlockSpec((k, tn), lambda j, i: (0, j))],
        out_specs=pl.BlockSpec((tm, tn), lambda j, i: (i, j)),
        out_shape=jax.ShapeDtypeStruct((m, n), out_dtype),
        compiler_params=_params("parallel", "parallel"),
        name="mm",
    )(x, w)


def _mm_res_kernel(r_ref, *refs):
    o_ref = refs[-1]
    n_ops = (len(refs) - 1) // 2
    acc = r_ref[...]
    for x_ref, w_ref in zip(refs[:n_ops], refs[n_ops:2 * n_ops]):
        acc = acc + jnp.dot(x_ref[...], w_ref[...], preferred_element_type=F32)
    o_ref[...] = acc


def mm_residual(r, xs, ws, tm, tn):
    m, n = r.shape
    return pl.pallas_call(
        _mm_res_kernel,
        grid=(n // tn, m // tm),
        in_specs=[pl.BlockSpec((tm, tn), lambda j, i: (i, j))]
        + [pl.BlockSpec((tm, x.shape[1]), lambda j, i: (i, 0)) for x in xs]
        + [pl.BlockSpec((w.shape[0], tn), lambda j, i: (0, j)) for w in ws],
        out_specs=pl.BlockSpec((tm, tn), lambda j, i: (i, j)),
        out_shape=jax.ShapeDtypeStruct((m, n), F32),
        compiler_params=_params("parallel", "parallel"),
        name="mm_res",
    )(r, *xs, *ws)


def _pick(n, cands):
    for c in cands:
        if n % c == 0:
            return c
    raise ValueError(n)


def mm_tokens(x, w, out_dtype=F32):
    t = x.shape[0]
    n = w.shape[1]
    return mm(x, w, _pick(t, (512, 256, 128)), _pick(n, (1024, 512, 256, 128)), out_dtype)


def _topk_rows(arrays, k):
    n, lanes = arrays[0].shape
    iota = lax.broadcasted_iota(jnp.int32, (n, lanes), 0)
    iota_k = lax.broadcasted_iota(jnp.int32, (k, lanes), 0)
    unranked = jnp.full((n, lanes), k, jnp.int32)
    zeros_k = jnp.zeros((k, lanes), F32)

    def run(exact_ties):
        carries_rank = [exact_ties or i == len(arrays) - 1 for i in range(len(arrays))]

        def body(r, carry):
            out = []
            for (w, rank, vals), has_rank in zip(carry, carries_rank):
                mx = jnp.max(w, axis=0, keepdims=True)
                hit = w == mx
                if exact_ties:
                    hit = iota == jnp.min(jnp.where(hit, iota, n), axis=0, keepdims=True)
                out.append((jnp.where(hit, NEG_INF, w), jnp.where(hit, r, rank) if has_rank else rank,
                            jnp.where(iota_k == r, mx, vals)))
            return tuple(out)

        init = tuple((s, unranked if has_rank else 0, zeros_k) for s, has_rank in zip(arrays, carries_rank))
        res = []
        for s, (_, rank, vals), has_rank in zip(arrays, lax.fori_loop(0, k, body, init), carries_rank):
            if not has_rank:
                rank = unranked
                for r in range(k):
                    rank = jnp.where(s == vals[r:r + 1, :], r, rank)
            res.append((rank, vals))
        return tuple(res)

    fast = run(False)
    ranked = sum(jnp.sum((rank < k).astype(jnp.int32), axis=0, keepdims=True) for rank, _ in fast)
    clean = jnp.max(ranked) == k * len(arrays)
    res = lax.cond(clean, lambda: fast, lambda: run(True))
    return [(vals, rank) for rank, vals in res]


def _peer_select_kernel(q_ref, keys_ref, rank2_ref, e2z_ref, nsel_ref, e1_ref):
    k = PEER_TOPK
    half = k // 2
    L = q_ref.shape[1]
    dk2 = PEER_DK // 2
    sub = lax.broadcasted_iota(jnp.int32, (half, L), 0)

    def head(h, _):
        base = pl.multiple_of(h * PEER_DK, PEER_DK)
        s1 = jnp.dot(keys_ref[2 * h], q_ref[pl.ds(base, dk2), :], preferred_element_type=F32)
        s2 = jnp.dot(keys_ref[2 * h + 1], q_ref[pl.ds(base + dk2, dk2), :], preferred_element_type=F32)
        (v1, rank1), (v2, rank2) = _topk_rows([s1, s2], k)
        pieces, pos = [], []
        for r1 in range(half):
            pieces.append(v1[r1:r1 + 1, :] + v2[0:half, :])
            pos.append(sub + r1 * k)
        pieces.append(v1[0:1, :] + v2[half:k, :])
        pos.append(sub + half)
        pieces.append(v1[half:k, :] + v2[0:1, :])
        pos.append((sub + half) * k)
        cand = jnp.concatenate(pieces, axis=0)
        posid = jnp.concatenate(pos, axis=0)
        big = k * k

        def pick_all(exact_ties):
            def pick(_, w):
                mx = jnp.max(w, axis=0, keepdims=True)
                hit = w == mx
                if exact_ties:
                    hit = posid == jnp.min(jnp.where(hit, posid, big), axis=0, keepdims=True)
                return jnp.where(hit, NEG_INF, w)

            return (lax.fori_loop(0, k, pick, cand) == NEG_INF).astype(F32)

        sel_fast = pick_all(False)
        picked = jnp.sum(sel_fast, axis=0, keepdims=True)
        sel = lax.cond(jnp.max(picked) == k, lambda: sel_fast, lambda: pick_all(True))
        top = v1[0:1, :] + v2[0:1, :]
        z = jnp.sum(sel * jnp.exp(cand - top), axis=0, keepdims=True)
        counts = []
        for r1 in range(half):
            c = jnp.sum(sel[r1 * half:(r1 + 1) * half, :], axis=0, keepdims=True)
            if r1 == 0:
                c = c + jnp.sum(sel[half * half:half * half + half, :], axis=0, keepdims=True)
            counts.append(c)
        tail = sel[half * half + half:, :]
        for r1 in range(half, k):
            counts.append(tail[r1 - half:r1 - half + 1, :])
        nsel = jnp.zeros(s1.shape, F32)
        for r1 in range(k):
            nsel = jnp.where(rank1 == r1, counts[r1], nsel)
        rank2_ref[h] = rank2.astype(F32)
        nsel_ref[h] = nsel
        e1_ref[h] = jnp.exp(s1 - v1[0:1, :])
        e2z_ref[h] = jnp.exp(s2 - v2[0:1, :]) / z
        return 0

    lax.fori_loop(0, PEER_HEADS, head, 0)


def peer_select(q_t, keys):
    rows, t = q_t.shape
    spec = pl.BlockSpec((PEER_HEADS, PEER_NKEYS, LANES), lambda i: (0, 0, i))
    shp = jax.ShapeDtypeStruct((PEER_HEADS, PEER_NKEYS, t), F32)
    return pl.pallas_call(
        _peer_select_kernel,
        grid=(t // LANES,),
        in_specs=[pl.BlockSpec((rows, LANES), lambda i: (0, i)),
                  pl.BlockSpec(keys.shape, lambda i: (0, 0, 0))],
        out_specs=[spec] * 4,
        out_shape=[shp] * 4,
        compiler_params=_params("parallel"),
        name="peer_select",
    )(q_t, keys)


def _gelu(x):
    return 0.5 * x * (1.0 + lax.erf(x * (1.0 / math.sqrt(2.0))))


def _peer_dense_kernel(xn_ref, u_ref, vt_ref, rank2_ref, e2z_ref, nsel_ref, e1_ref, o_ref, a_ref):
    j = pl.program_id(1)
    nb = u_ref.shape[0] // PEER_NKEYS

    @pl.when(j == 0)
    def _():
        o_ref[...] = jnp.zeros_like(o_ref)

    for kb in range(nb):
        a = j * nb + kb
        rows = slice(kb * PEER_NKEYS, (kb + 1) * PEER_NKEYS)
        s = jnp.dot(u_ref[rows, :], xn_ref[...], preferred_element_type=F32)
        g = jnp.zeros(s.shape, F32)
        for h in range(PEER_HEADS):
            n_row = nsel_ref[h, pl.ds(a, 1), :]
            e_row = e1_ref[h, pl.ds(a, 1), :]
            g = g + jnp.where(rank2_ref[h] < n_row, e2z_ref[h], 0.0) * e_row
        a_ref[rows, :] = (_gelu(s) * g).astype(BF16)
    o_ref[...] += jnp.dot(vt_ref[...], a_ref[...], preferred_element_type=F32)


def peer_dense(xn_t, u_bf, vt_bf, rank2, e2z, nsel, e1, tm=512, te=512):
    d, t = xn_t.shape
    n_exp = u_bf.shape[0]
    sel_spec = pl.BlockSpec((PEER_HEADS, PEER_NKEYS, tm), lambda i, j: (0, 0, i))
    return pl.pallas_call(
        _peer_dense_kernel,
        grid=(t // tm, n_exp // te),
        in_specs=[pl.BlockSpec((d, tm), lambda i, j: (0, i)),
                  pl.BlockSpec((te, d), lambda i, j: (j, 0)),
                  pl.BlockSpec((d, te), lambda i, j: (0, j)),
                  sel_spec, sel_spec, sel_spec, sel_spec],
        out_specs=pl.BlockSpec((d, tm), lambda i, j: (0, i)),
        out_shape=jax.ShapeDtypeStruct((d, t), F32),
        scratch_shapes=[pltpu.VMEM((te, tm), BF16)],
        compiler_params=_params("parallel", "arbitrary"),
        name="peer_dense",
    )(xn_t, u_bf, vt_bf, rank2, e2z, nsel, e1)


def peer(x, g, w_q, sub_keys, u_tab, v_tab):
    t, d = x.shape
    xn_t = rmsnorm_bf16_t(x, g)
    wq_t = w_q.T.astype(BF16)
    q_t = mm(wq_t, xn_t, _pick(wq_t.shape[0], (512, 256, 128)), _pick(t, (512, 256, 128)), BF16)
    keys = sub_keys.reshape(PEER_HEADS * 2, PEER_NKEYS, PEER_DK // 2).astype(BF16)
    rank2, e2z, nsel, e1 = peer_select(q_t, keys)
    out_t = peer_dense(xn_t, u_tab.astype(BF16), cast_transposed_bf16(v_tab), rank2, e2z, nsel, e1,
                       tm=_pick(t, (512, 256, 128)))
    return add_transposed(x, out_t)


def _fox_norm_kernel(p_ref, gq_ref, gk_ref, q_ref, k_ref):
    scale = FOX_DH ** -0.5
    for h in range(FOX_HEADS):
        lo, hi = h * FOX_DH, (h + 1) * FOX_DH
        x = p_ref[:, lo:hi]
        y = x * lax.rsqrt(jnp.mean(x * x, axis=-1, keepdims=True) + RMS_EPS) * gq_ref[...]
        q_ref[:, lo:hi] = (y * scale).astype(q_ref.dtype)
        x = p_ref[:, FOX_W + lo:FOX_W + hi]
        k_ref[:, lo:hi] = x * lax.rsqrt(jnp.mean(x * x, axis=-1, keepdims=True) + RMS_EPS) * gk_ref[...]


def fox_norm(p, g_q, g_k):
    t = p.shape[0]
    tm = _pick(t, (256, 128))
    return pl.pallas_call(
        _fox_norm_kernel,
        grid=(t // tm,),
        in_specs=[pl.BlockSpec((tm, 2 * FOX_W), lambda i: (i, 0)),
                  pl.BlockSpec((1, FOX_DH), lambda i: (0, 0)), pl.BlockSpec((1, FOX_DH), lambda i: (0, 0))],
        out_specs=[pl.BlockSpec((tm, FOX_W), lambda i: (i, 0))] * 2,
        out_shape=[jax.ShapeDtypeStruct((t, FOX_W), BF16), jax.ShapeDtypeStruct((t, FOX_W), F32)],
        compiler_params=_params("parallel"),
        name="fox_norm",
    )(p, g_q.reshape(1, FOX_DH), g_k.reshape(1, FOX_DH))


def _fox_prompt_kernel(qi_ref, ki_ref, q_ref, k_ref, v_ref, f_ref, o_ref, m_ref, l_ref, acc_ref, *, tq, tk):
    n = pl.program_id(2)
    i = qi_ref[n]
    kk = ki_ref[n]

    @pl.when(kk == 0)
    def _():
        m_ref[...] = jnp.full_like(m_ref, NEG_INF)
        l_ref[...] = jnp.zeros_like(l_ref)
        acc_ref[...] = jnp.zeros_like(acc_ref)

    def step(masked):
        s = lax.dot_general(q_ref[...], k_ref[...].astype(BF16), (((1,), (1,)), ((), ())),
                            preferred_element_type=F32)
        s = s - f_ref[0]
        if masked:
            q_pos = i * tq + lax.broadcasted_iota(jnp.int32, s.shape, 0)
            k_pos = kk * tk + lax.broadcasted_iota(jnp.int32, s.shape, 1)
            s = jnp.where(q_pos >= k_pos, s, NEG_INF)
        m_prev = m_ref[...]
        m_new = jnp.maximum(m_prev, jnp.max(s, axis=1, keepdims=True))
        alpha = jnp.exp(m_prev - m_new)
        p = jnp.exp(s - m_new)
        l_ref[...] = alpha * l_ref[...] + jnp.sum(p, axis=1, keepdims=True)
        acc_ref[...] = alpha * acc_ref[...] + jnp.dot(p.astype(BF16), v_ref[...].astype(BF16),
                                                      preferred_element_type=F32)
        m_ref[...] = m_new

    on_diag = (kk + 1) * tk > i * tq + 1

    @pl.when(on_diag)
    def _():
        step(True)

    @pl.when(jnp.logical_not(on_diag))
    def _():
        step(False)

    @pl.when(kk == (i * tq + tq - 1) // tk)
    def _():
        o_ref[...] = (acc_ref[...] / l_ref[...]).astype(o_ref.dtype)


def fox_prompt(q_bf, k, p, f_cum, bsz, s_len, tq=1024, tk=1024):
    nq, nk = s_len // tq, s_len // tk
    pairs = [(i, kk) for i in range(nq) for kk in range((i * tq + tq - 1) // tk + 1)]
    qi = jnp.asarray([a for a, _ in pairs], jnp.int32)
    ki = jnp.asarray([b for _, b in pairs], jnp.int32)
    v_col0 = 2 * FOX_W // FOX_DH
    grid_spec = pltpu.PrefetchScalarGridSpec(
        num_scalar_prefetch=2,
        grid=(bsz, FOX_HEADS, len(pairs)),
        in_specs=[pl.BlockSpec((tq, FOX_DH), lambda b, h, n, qi, ki: (b * nq + qi[n], h)),
                  pl.BlockSpec((tk, FOX_DH), lambda b, h, n, qi, ki: (b * nk + ki[n], h)),
                  pl.BlockSpec((tk, FOX_DH), lambda b, h, n, qi, ki: (b * nk + ki[n], v_col0 + h)),
                  pl.BlockSpec((1, 1, tk), lambda b, h, n, qi, ki: (b * FOX_HEADS + h, 0, ki[n]))],
        out_specs=pl.BlockSpec((tq, FOX_DH), lambda b, h, n, qi, ki: (b * nq + qi[n], h)),
        scratch_shapes=[pltpu.VMEM((tq, 1), F32), pltpu.VMEM((tq, 1), F32), pltpu.VMEM((tq, FOX_DH), F32)],
    )
    return pl.pallas_call(
        functools.partial(_fox_prompt_kernel, tq=tq, tk=tk),
        grid_spec=grid_spec,
        out_shape=jax.ShapeDtypeStruct((bsz * s_len, FOX_W), BF16),
        compiler_params=_params("parallel", "parallel", "arbitrary"),
        name="fox_prompt",
    )(qi, ki, q_bf, k, p, f_cum)


SUB = 8


def _logf_cumsum_kernel(pt_ref, *refs):
    lf_refs, o_ref = refs[:-1], refs[-1]
    page, nh = lf_refs[0].shape[1:]
    tri = (lax.broadcasted_iota(jnp.int32, (page, page), 1)
           <= lax.broadcasted_iota(jnp.int32, (page, page), 0)).astype(F32)
    carry = jnp.zeros((1, nh), F32)
    for p, lf_ref in enumerate(lf_refs):
        f = jnp.dot(tri, lf_ref[0], precision=lax.Precision.HIGHEST, preferred_element_type=F32) + carry
        carry = f[page - 1:page, :]
        o_ref[0, p] = f


def paged_logf_cumsum(cache_logf, page_table):
    bsz, n_pages = page_table.shape
    _, page, nh = cache_logf.shape

    def page_spec(p):
        return pl.BlockSpec((1, page, nh), lambda b, pt: (pt[b, p], 0, 0))

    grid_spec = pltpu.PrefetchScalarGridSpec(
        num_scalar_prefetch=1,
        grid=(bsz,),
        in_specs=[page_spec(p) for p in range(n_pages)],
        out_specs=pl.BlockSpec((1, n_pages, page, nh), lambda b, pt: (b, 0, 0, 0)),
    )
    return pl.pallas_call(
        _logf_cumsum_kernel,
        grid_spec=grid_spec,
        out_shape=jax.ShapeDtypeStruct((bsz, n_pages, page, nh), F32),
        compiler_params=_params("parallel"),
        name="logf_cumsum",
    )(page_table, *([cache_logf] * n_pages))


def _fox_sample_kernel(pt_ref, q_ref, *refs, n_groups, group):
    kc_refs, vc_refs = refs[:group], refs[group:2 * group]
    f_ref, mask_ref, kn_ref, vn_ref, fn_ref, maskn_ref, o_ref, m_ref, l_ref, acc_ref = refs[2 * group:]
    pg = pl.program_id(1)

    @pl.when(pg == 0)
    def _():
        m_ref[...] = jnp.full_like(m_ref, NEG_INF)
        l_ref[...] = jnp.zeros_like(l_ref)
        acc_ref[...] = jnp.zeros_like(acc_ref)

    def attend(kvb):
        ss = [lax.dot_general(q_ref[0], k2, (((1,), (1,)), ((), ())), preferred_element_type=F32) + b
              for k2, _, b in kvb]
        m_prev = m_ref[...]
        m_new = m_prev
        for s in ss:
            m_new = jnp.maximum(m_new, jnp.max(s, axis=1, keepdims=True))
        alpha = jnp.exp(m_prev - m_new)
        l_new = alpha * l_ref[...]
        acc = alpha * acc_ref[...]
        for s, (_, v2, _) in zip(ss, kvb):
            p = jnp.exp(s - m_new)
            l_new = l_new + jnp.sum(p, axis=1, keepdims=True)
            acc = acc + jnp.dot(p.astype(BF16), v2, preferred_element_type=F32)
        l_ref[...] = l_new
        acc_ref[...] = acc
        m_ref[...] = m_new

    @pl.when(pg < n_groups)
    def _():
        rows = kc_refs[0].shape[0] * kc_refs[0].shape[1]
        attend([(kc[...].reshape(rows, FOX_DH).astype(BF16), vc[...].reshape(rows, FOX_DH).astype(BF16),
                 mask_ref[...] - f_ref[0, g])
                for g, (kc, vc) in enumerate(zip(kc_refs, vc_refs))])

    @pl.when(pg == n_groups)
    def _():
        attend([(kn_ref[0].astype(BF16), vn_ref[0].astype(BF16), maskn_ref[...] - fn_ref[0])])
        o_ref[0] = acc_ref[...] / l_ref[...]


def fox_sample(q_bf, k_new, v_new, lf_new, cache_k, cache_v, cache_logf, page_table):
    bsz, t_len, _ = q_bf.shape
    n_pool, page = cache_k.shape[:2]
    n_pages = page_table.shape[1]
    nh, dh = FOX_HEADS, FOX_DH
    rows = nh * t_len
    lanes = page * nh
    lanes_new = SUB * nh
    qf = q_bf.reshape(bsz, t_len, nh, dh).transpose(0, 2, 1, 3).reshape(bsz, rows, dh)
    f_pages = paged_logf_cumsum(cache_logf, page_table)
    f_rows = f_pages.reshape(bsz, n_pages, 1, lanes)
    pad = ((0, 0), (0, SUB - t_len), (0, 0))
    kn = jnp.pad(k_new, pad).reshape(bsz, lanes_new, dh)
    vn = jnp.pad(v_new, pad).reshape(bsz, lanes_new, dh)
    fn = (f_pages[:, -1, -1, :][:, None, :] + jnp.cumsum(jnp.pad(lf_new, pad), axis=1)).reshape(bsz, 1, lanes_new)
    row_head, row_q = np.arange(rows) // t_len, np.arange(rows) % t_len
    mask = np.where(row_head[:, None] == (np.arange(lanes) % nh)[None, :], 0.0, NEG_INF).astype(np.float32)
    key_n, head_n = np.arange(lanes_new) // nh, np.arange(lanes_new) % nh
    ok = (row_head[:, None] == head_n[None, :]) & (key_n[None, :] <= row_q[:, None]) & (key_n[None, :] < t_len)
    mask_new = np.where(ok, 0.0, NEG_INF).astype(np.float32)
    group = _pick(n_pages, (4, 2, 1))
    n_groups = n_pages // group
    last = n_groups - 1

    def cache_spec(g):
        return pl.BlockSpec((page, nh, dh), lambda b, p, pt: (pt[b, jnp.minimum(p, last) * group + g], 0, 0))

    cache_specs = [cache_spec(g) for g in range(group)]
    kc3 = cache_k.reshape(n_pool * page, nh, dh)
    vc3 = cache_v.reshape(n_pool * page, nh, dh)
    grid_spec = pltpu.PrefetchScalarGridSpec(
        num_scalar_prefetch=1,
        grid=(bsz, n_groups + 1),
        in_specs=[pl.BlockSpec((1, rows, dh), lambda b, p, pt: (b, 0, 0)),
                  *cache_specs, *cache_specs,
                  pl.BlockSpec((1, group, 1, lanes), lambda b, p, pt: (b, jnp.minimum(p, last), 0, 0)),
                  pl.BlockSpec((rows, lanes), lambda b, p, pt: (0, 0)),
                  pl.BlockSpec((1, lanes_new, dh), lambda b, p, pt: (b, 0, 0)),
                  pl.BlockSpec((1, lanes_new, dh), lambda b, p, pt: (b, 0, 0)),
                  pl.BlockSpec((1, 1, lanes_new), lambda b, p, pt: (b, 0, 0)),
                  pl.BlockSpec((rows, lanes_new), lambda b, p, pt: (0, 0))],
        out_specs=pl.BlockSpec((1, rows, dh), lambda b, p, pt: (b, 0, 0)),
        scratch_shapes=[pltpu.VMEM((rows, 1), F32), pltpu.VMEM((rows, 1), F32), pltpu.VMEM((rows, dh), F32)],
    )
    out = pl.pallas_call(
        functools.partial(_fox_sample_kernel, n_groups=n_groups, group=group),
        grid_spec=grid_spec,
        out_shape=jax.ShapeDtypeStruct((bsz, rows, dh), F32),
        compiler_params=_params("parallel", "arbitrary"),
        name="fox_sample",
    )(page_table, qf, *([kc3] * group), *([vc3] * group),
      f_rows, jnp.asarray(mask), kn, vn, fn, jnp.asarray(mask_new))
    return out.reshape(bsz, nh, t_len, dh).transpose(0, 2, 1, 3).reshape(bsz, t_len, nh * dh)


def _mlstm_kernel(q_ref, k_ref, v_ref, o_ref, g_ref, c0_ref, n0_ref, m0_ref, mg_ref,
                  h_ref, c1_ref, n1_ref, m1_ref, c_scr, n_scr, m_scr):
    c = pl.program_id(1)
    L = q_ref.shape[0]
    nh, dh = MLSTM_HEADS, MLSTM_DH

    @pl.when(c == 0)
    def _():
        c_scr[...] = c0_ref[0]
        n_scr[...] = n0_ref[0]
        m_scr[...] = m0_ref[0]

    t_idx = lax.broadcasted_iota(jnp.int32, (L, L), 0)
    s_idx = lax.broadcasted_iota(jnp.int32, (L, L), 1)
    causal = s_idx <= t_idx
    diag = s_idx == t_idx
    for h in range(nh):
        cols = slice(h * dh, (h + 1) * dh)
        ig = g_ref[0, h:h + 1, :]
        lf = g_ref[0, nh + h:nh + h + 1, :]
        m0 = m_scr[h:h + 1, :]
        f_col = jnp.sum(jnp.where(causal, lf, 0.0), axis=1, keepdims=True)
        ig_col = jnp.sum(jnp.where(diag, ig, 0.0), axis=1, keepdims=True)
        g_col = ig_col - f_col
        g_row = jnp.sum(jnp.where(diag, g_col, 0.0), axis=0, keepdims=True)
        cm_col = jnp.max(jnp.where(causal, g_row, NEG_INF), axis=1, keepdims=True)
        m_col = f_col + jnp.maximum(m0, cm_col)
        b_col = f_col - m_col
        a_col = jnp.exp(f_col + m0 - m_col)
        d = jnp.exp(jnp.where(causal, b_col + g_row, NEG_INF))
        q = q_ref[:, cols] * (dh ** -0.5)
        k = k_ref[:, cols]
        v = v_ref[:, cols]
        qb, kb, vb = q.astype(BF16), k.astype(BF16), v.astype(BF16)
        qk = lax.dot_general(qb, kb, (((1,), (1,)), ((), ())), preferred_element_type=F32) * d
        c_old = c_scr[h]
        n_old = n_scr[h:h + 1, :]
        num = a_col * jnp.dot(qb, c_old.astype(BF16), preferred_element_type=F32) \
            + jnp.dot(qk.astype(BF16), vb, preferred_element_type=F32)
        den = a_col * jnp.sum(q * n_old, axis=1, keepdims=True) + jnp.sum(qk, axis=1, keepdims=True)
        hs = num / jnp.maximum(jnp.abs(den), jnp.exp(-m_col))
        hn = hs * lax.rsqrt(jnp.mean(hs * hs, axis=1, keepdims=True) + RMS_EPS) * mg_ref[:, cols]
        h_ref[:, cols] = (hn * jax.nn.sigmoid(o_ref[:, cols])).astype(h_ref.dtype)
        b_last = b_col[L - 1:L, :]
        a_last = a_col[L - 1:L, :]
        kd = k * jnp.exp(b_last + g_col)
        c_scr[h] = a_last * c_old + lax.dot_general(kd.astype(BF16), vb, (((0,), (0,)), ((), ())),
                                                     preferred_element_type=F32)
        n_scr[h:h + 1, :] = a_last * n_old + jnp.sum(kd, axis=0, keepdims=True)
        m_scr[h:h + 1, :] = m_col[L - 1:L, :]

    @pl.when(c == pl.num_programs(1) - 1)
    def _():
        c1_ref[0] = c_scr[...]
        n1_ref[0] = n_scr[...]
        m1_ref[0] = m_scr[...]


def mlstm(p, gates, c0, n0, m0, mh_g, bsz, n_chunks, chunk, col0):
    nh, dh, mw = MLSTM_HEADS, MLSTM_DH, MLSTM_W
    rows = bsz * n_chunks * chunk

    def col_spec(j):
        return pl.BlockSpec((chunk, mw), lambda b, c: (b * n_chunks + c, col0 + j))

    state = lambda b, c: (b, 0, 0)
    return pl.pallas_call(
        _mlstm_kernel,
        grid=(bsz, n_chunks),
        in_specs=[col_spec(0), col_spec(1), col_spec(2), col_spec(3),
                  pl.BlockSpec((1, 2 * nh, chunk), lambda b, c: (b * n_chunks + c, 0, 0)),
                  pl.BlockSpec((1, nh, dh, dh), lambda b, c: (b, 0, 0, 0)),
                  pl.BlockSpec((1, nh, dh), state), pl.BlockSpec((1, nh, 1), state),
                  pl.BlockSpec((1, mw), lambda b, c: (0, 0))],
        out_specs=[pl.BlockSpec((chunk, mw), lambda b, c: (b * n_chunks + c, 0)),
                   pl.BlockSpec((1, nh, dh, dh), lambda b, c: (b, 0, 0, 0)),
                   pl.BlockSpec((1, nh, dh), state), pl.BlockSpec((1, nh, 1), state)],
        out_shape=[jax.ShapeDtypeStruct((rows, mw), BF16),
                   jax.ShapeDtypeStruct((bsz, nh, dh, dh), F32),
                   jax.ShapeDtypeStruct((bsz, nh, dh), F32),
                   jax.ShapeDtypeStruct((bsz, nh, 1), F32)],
        scratch_shapes=[pltpu.VMEM((nh, dh, dh), F32), pltpu.VMEM((nh, dh), F32), pltpu.VMEM((nh, 1), F32)],
        compiler_params=_params("parallel", "arbitrary"),
        name="mlstm",
    )(p, p, p, p, gates, c0, n0, m0.reshape(bsz, nh, 1), mh_g.reshape(1, mw))


CONV_HALO = 32
CONV_ROWS = 32


def _group_ln_swish(y, g, b):
    outs = []
    for grp in range(CONV_GROUPS):
        x = y[:, grp * LANES:(grp + 1) * LANES]
        mu = jnp.mean(x, axis=1, keepdims=True)
        xc = x - mu
        var = jnp.mean(xc * xc, axis=1, keepdims=True)
        z = xc * lax.rsqrt(var + LN_EPS) * g[:, grp * LANES:(grp + 1) * LANES] + b[:, grp * LANES:(grp + 1) * LANES]
        outs.append(z * jax.nn.sigmoid(z))
    return jnp.concatenate(outs, axis=1)


def _conv_prompt_kernel(cv_ref, cg_ref, hv_ref, hg_ref, w_ref, b_ref, g_ref, beta_ref, y_ref, buf_ref, ext_ref):
    i = pl.program_id(1)
    ts = cv_ref.shape[0]
    hist = jnp.where(i > 0, hv_ref[...] * jax.nn.sigmoid(hg_ref[...]), 0.0)
    ext_ref[0:CONV_HALO, :] = hist
    ext_ref[CONV_HALO:, :] = cv_ref[...] * jax.nn.sigmoid(cg_ref[...])
    off = CONV_HALO - (CONV_K - 1)
    for r0 in range(0, ts, CONV_ROWS):
        acc = jnp.zeros((CONV_ROWS, CONV_W), F32) + b_ref[...]
        for j in range(CONV_K):
            acc = acc + ext_ref[r0 + off + j:r0 + off + j + CONV_ROWS, :] * w_ref[j:j + 1, :]
        y_ref[r0:r0 + CONV_ROWS, :] = _group_ln_swish(acc, g_ref[...], beta_ref[...]).astype(y_ref.dtype)

    @pl.when(i == pl.num_programs(1) - 1)
    def _():
        buf_ref[0] = ext_ref[ts + off:ts + CONV_HALO, :]


def conv_prompt(p, conv_w, conv_b, cn_g, cn_b, bsz, s_len, ts=512):
    nt = s_len // ts
    hb = ts // CONV_HALO

    def halo(col):
        return pl.BlockSpec((CONV_HALO, CONV_W), lambda b, i: (jnp.maximum((b * nt + i) * hb - 1, 0), col))

    vec = pl.BlockSpec((1, CONV_W), lambda b, i: (0, 0))
    return pl.pallas_call(
        _conv_prompt_kernel,
        grid=(bsz, nt),
        in_specs=[pl.BlockSpec((ts, CONV_W), lambda b, i: (b * nt + i, 0)),
                  pl.BlockSpec((ts, CONV_W), lambda b, i: (b * nt + i, 1)),
                  halo(0), halo(1),
                  pl.BlockSpec((CONV_K, CONV_W), lambda b, i: (0, 0)), vec, vec, vec],
        out_specs=[pl.BlockSpec((ts, CONV_W), lambda b, i: (b * nt + i, 0)),
                   pl.BlockSpec((1, CONV_K - 1, CONV_W), lambda b, i: (b, 0, 0))],
        out_shape=[jax.ShapeDtypeStruct((bsz * s_len, CONV_W), BF16),
                   jax.ShapeDtypeStruct((bsz, CONV_K - 1, CONV_W), F32)],
        scratch_shapes=[pltpu.VMEM((ts + CONV_HALO, CONV_W), F32)],
        compiler_params=_params("parallel", "arbitrary"),
        name="conv_prompt",
    )(p, p, p, p, conv_w, conv_b.reshape(1, -1), cn_g.reshape(1, -1), cn_b.reshape(1, -1))


def _conv_sample_kernel(cv_ref, cg_ref, st_ref, w_ref, b_ref, g_ref, beta_ref, y_ref, buf_ref, ext_ref, *, t_len):
    nb = st_ref.shape[0]
    hist = CONV_K - 1
    y_ref[...] = jnp.zeros_like(y_ref)
    for bb in range(nb):
        rows = slice(bb * SUB, bb * SUB + t_len)
        ext_ref[0:hist, :] = st_ref[bb]
        ext_ref[hist:hist + t_len, :] = cv_ref[rows, :] * jax.nn.sigmoid(cg_ref[rows, :])
        ys = [jnp.sum(ext_ref[t:t + CONV_K, :] * w_ref[...], axis=0, keepdims=True) for t in range(t_len)]
        y = jnp.concatenate(ys, axis=0) + b_ref[...]
        y_ref[rows, :] = _group_ln_swish(y, g_ref[...], beta_ref[...]).astype(y_ref.dtype)
        buf_ref[bb] = ext_ref[t_len:t_len + hist, :]


def conv_sample(p_pad, state_conv, conv_w, conv_b, cn_g, cn_b, t_len, nb=8):
    bsz = state_conv.shape[0]
    vec = pl.BlockSpec((1, CONV_W), lambda i: (0, 0))
    return pl.pallas_call(
        functools.partial(_conv_sample_kernel, t_len=t_len),
        grid=(bsz // nb,),
        in_specs=[pl.BlockSpec((nb * SUB, CONV_W), lambda i: (i, 0)),
                  pl.BlockSpec((nb * SUB, CONV_W), lambda i: (i, 1)),
                  pl.BlockSpec((nb, CONV_K - 1, CONV_W), lambda i: (i, 0, 0)),
                  pl.BlockSpec((CONV_K, CONV_W), lambda i: (0, 0)), vec, vec, vec],
        out_specs=[pl.BlockSpec((nb * SUB, CONV_W), lambda i: (i, 0)),
                   pl.BlockSpec((nb, CONV_K - 1, CONV_W), lambda i: (i, 0, 0))],
        out_shape=[jax.ShapeDtypeStruct((bsz * SUB, CONV_W), BF16),
                   jax.ShapeDtypeStruct((bsz, CONV_K - 1, CONV_W), F32)],
        scratch_shapes=[pltpu.VMEM((CONV_K - 1 + SUB, CONV_W), F32)],
        compiler_params=_params("parallel"),
        name="conv_sample",
    )(p_pad, p_pad, state_conv, conv_w, conv_b.reshape(1, -1), cn_g.reshape(1, -1), cn_b.reshape(1, -1))


def kernel(x_prompt, x_sample, state_conv, state_C, state_n, state_m, cache_k, cache_v, cache_logf, page_table,
           norm0_mix, w_in0, b_ig0, b_fg0, conv_w0, conv_b0, conv_norm_g0, conv_norm_b0, mlstm_norm_g0, w_out0,
           norm1_mix, w_in1, b_f1, q_norm_g1, k_norm_g1, w_out1,
           norm0_ffn, peer_wq0, peer_keys0, peer_u0, peer_v0,
           norm1_ffn, peer_wq1, peer_keys1, peer_u1, peer_v1):
    bp, sp, d = x_prompt.shape
    bs, ss, _ = x_sample.shape
    tp, ts = bp * sp, bs * ss
    x = jnp.concatenate([x_prompt.reshape(tp, d), x_sample.reshape(ts, d)], axis=0)

    n_main0 = 2 * CONV_W + 4 * MLSTM_W
    xn = rmsnorm_bf16(x, norm0_mix)
    p0 = mm_tokens(xn, w_in0[:, :n_main0].astype(BF16))
    g0 = gate_proj(x, norm0_mix, w_in0[:, n_main0:])
    nh = MLSTM_HEADS
    gates = jnp.concatenate([g0[:, :nh] + b_ig0, jax.nn.log_sigmoid(g0[:, nh:] + b_fg0)], axis=1)
    conv_par = (conv_w0, conv_b0, conv_norm_g0, conv_norm_b0)
    mcol0 = 2 * CONV_W // MLSTM_W
    nc = sp // MLSTM_CHUNK
    gates_p = gates[:tp].reshape(bp * nc, MLSTM_CHUNK, 2 * nh).transpose(0, 2, 1)
    hp, p_C, p_n, p_m = mlstm(p0, gates_p, jnp.zeros((bp, nh, MLSTM_DH, MLSTM_DH), F32),
                              jnp.zeros((bp, nh, MLSTM_DH), F32), jnp.zeros((bp, nh), F32),
                              mlstm_norm_g0, bp, nc, MLSTM_CHUNK, mcol0)
    yp, p_conv = conv_prompt(p0, *conv_par, bp, sp)
    tpad = ((0, 0), (0, SUB - ss), (0, 0))
    p0s = jnp.pad(p0[tp:].reshape(bs, ss, -1), tpad).reshape(bs * SUB, -1)
    gs = gates[tp:].reshape(bs, ss, 2 * nh)
    gates_s = jnp.concatenate([jnp.pad(gs[..., :nh], tpad, constant_values=NEG_INF),
                               jnp.pad(gs[..., nh:], tpad)], axis=-1).transpose(0, 2, 1)
    hs, s_C, s_n, s_m = mlstm(p0s, gates_s, state_C, state_n, state_m, mlstm_norm_g0, bs, 1, SUB, mcol0)
    ys, s_conv = conv_sample(p0s, state_conv, *conv_par, ss)
    unpad = lambda a: a.reshape(bs, SUB, -1)[:, :ss].reshape(ts, -1)
    y_all = jnp.concatenate([yp, unpad(ys)], axis=0)
    h_all = jnp.concatenate([hp[:tp], unpad(hs)], axis=0)
    w_out0_bf = w_out0.astype(BF16)
    tm_out = _pick(tp + ts, (512, 256, 128))
    x = mm_residual(x, [y_all, h_all], [w_out0_bf[:CONV_W], w_out0_bf[CONV_W:]], tm_out, 1024)
    p_m, s_m = p_m.reshape(bp, nh), s_m.reshape(bs, nh)
    x = peer(x, norm0_ffn, peer_wq0, peer_keys0, peer_u0, peer_v0)

    xn = rmsnorm_bf16(x, norm1_mix)
    p1 = mm_tokens(xn, w_in1[:, :3 * FOX_W].astype(BF16))
    g1 = gate_proj(x, norm1_mix, w_in1[:, 3 * FOX_W:])
    lf = jax.nn.log_sigmoid(g1 + b_f1)
    q_bf, k = fox_norm(p1, q_norm_g1, k_norm_g1)
    v = p1[:, 2 * FOX_W:]
    f_cum = jnp.cumsum(lf[:tp].reshape(bp, sp, FOX_HEADS), axis=1).transpose(0, 2, 1).reshape(bp * FOX_HEADS, 1, sp)
    attn_p = fox_prompt(q_bf, k, p1, f_cum, bp, sp)
    attn_s = fox_sample(q_bf[tp:].reshape(bs, ss, FOX_W), k[tp:].reshape(bs, ss, FOX_W),
                        v[tp:].reshape(bs, ss, FOX_W), lf[tp:].reshape(bs, ss, FOX_HEADS),
                        cache_k, cache_v, cache_logf, page_table)
    attn = jnp.concatenate([attn_p, attn_s.reshape(ts, FOX_W).astype(BF16)], axis=0)
    x = mm_residual(x, [attn], [w_out1.astype(BF16)], tm_out, 1024)
    x = peer(x, norm1_ffn, peer_wq1, peer_keys1, peer_u1, peer_v1)

    hd = (FOX_HEADS, FOX_DH)
    return (x[:tp].reshape(bp, sp, d), x[tp:].reshape(bs, ss, d), p_conv, p_C, p_n, p_m,
            k[:tp].reshape(bp, sp, *hd), v[:tp].reshape(bp, sp, *hd), lf[:tp].reshape(bp, sp, FOX_HEADS),
            s_conv, s_C, s_n, s_m,
            k[tp:].reshape(bs, ss, *hd), v[tp:].reshape(bs, ss, *hd), lf[tp:].reshape(bs, ss, FOX_HEADS))
```

```python
import functools
import math

import jax
import jax.numpy as jnp
import numpy as np
from jax import lax
from jax.experimental import pallas as pl
from jax.experimental.pallas import tpu as pltpu

F32 = jnp.float32
BF16 = jnp.bfloat16

D_MODEL = 2048
CONV_W = 1024
CONV_K = 31
CONV_GROUPS = 8
MLSTM_HEADS = 4
MLSTM_W = 1024
MLSTM_DH = 256
MLSTM_CHUNK = 128
FOX_HEADS = 16
FOX_DH = 128
FOX_W = 2048
PEER_HEADS = 8
PEER_NKEYS = 128
PEER_DK = 128
PEER_TOPK = 16
RMS_EPS = 1e-6
LN_EPS = 1e-5

LANES = 128
VMEM_LIMIT = 56 * 1024 * 1024
NEG_INF = float("-inf")


def _params(*sem):
    return pltpu.CompilerParams(dimension_semantics=sem, vmem_limit_bytes=VMEM_LIMIT)


def _rmsnorm_kernel(x_ref, g_ref, o_ref):
    x = x_ref[...]
    y = x * lax.rsqrt(jnp.mean(x * x, axis=-1, keepdims=True) + RMS_EPS)
    o_ref[...] = (y * g_ref[...]).astype(o_ref.dtype)


def rmsnorm_bf16(x, g):
    t, d = x.shape
    tm = _pick(t, (512, 256, 128))
    return pl.pallas_call(
        _rmsnorm_kernel,
        grid=(t // tm,),
        in_specs=[pl.BlockSpec((tm, d), lambda i: (i, 0)), pl.BlockSpec((1, d), lambda i: (0, 0))],
        out_specs=pl.BlockSpec((tm, d), lambda i: (i, 0)),
        out_shape=jax.ShapeDtypeStruct((t, d), BF16),
        compiler_params=_params("parallel"),
        name="rmsnorm",
    )(x, g.reshape(1, d))


def _rmsnorm_t_kernel(x_ref, g_ref, o_ref):
    x = x_ref[...]
    y = x * lax.rsqrt(jnp.mean(x * x, axis=-1, keepdims=True) + RMS_EPS) * g_ref[...]
    o_ref[...] = y.T.astype(o_ref.dtype)


def rmsnorm_bf16_t(x, g):
    t, d = x.shape
    tm = _pick(t, (512, 256, 128))
    return pl.pallas_call(
        _rmsnorm_t_kernel,
        grid=(t // tm,),
        in_specs=[pl.BlockSpec((tm, d), lambda i: (i, 0)), pl.BlockSpec((1, d), lambda i: (0, 0))],
        out_specs=pl.BlockSpec((d, tm), lambda i: (0, i)),
        out_shape=jax.ShapeDtypeStruct((d, t), BF16),
        compiler_params=_params("parallel"),
        name="rmsnorm_t",
    )(x, g.reshape(1, d))


def _cast_t_kernel(x_ref, o_ref):
    o_ref[...] = x_ref[...].T.astype(o_ref.dtype)


def cast_transposed_bf16(w):
    n, d = w.shape
    tn = _pick(n, (512, 256, 128))
    return pl.pallas_call(
        _cast_t_kernel,
        grid=(n // tn,),
        in_specs=[pl.BlockSpec((tn, d), lambda i: (i, 0))],
        out_specs=pl.BlockSpec((d, tn), lambda i: (0, i)),
        out_shape=jax.ShapeDtypeStruct((d, n), BF16),
        compiler_params=_params("parallel"),
        name="cast_t",
    )(w)


def _add_t_kernel(x_ref, yt_ref, o_ref):
    o_ref[...] = x_ref[...] + yt_ref[...].T


def add_transposed(x, y_t):
    t, d = x.shape
    tm = _pick(t, (512, 256, 128))
    return pl.pallas_call(
        _add_t_kernel,
        grid=(t // tm,),
        in_specs=[pl.BlockSpec((tm, d), lambda i: (i, 0)), pl.BlockSpec((d, tm), lambda i: (0, i))],
        out_specs=pl.BlockSpec((tm, d), lambda i: (i, 0)),
        out_shape=jax.ShapeDtypeStruct((t, d), F32),
        compiler_params=_params("parallel"),
        name="add_t",
    )(x, y_t)


def _gate_proj_kernel(x_ref, g_ref, w_ref, o_ref):
    x = x_ref[...]
    y = x * lax.rsqrt(jnp.mean(x * x, axis=-1, keepdims=True) + RMS_EPS) * g_ref[...]
    o_ref[...] = jnp.dot(y, w_ref[...], precision=lax.Precision.HIGHEST, preferred_element_type=F32)


def gate_proj(x, g, w):
    t, d = x.shape
    n = w.shape[1]
    tm = _pick(t, (256, 128))
    out = pl.pallas_call(
        _gate_proj_kernel,
        grid=(t // tm,),
        in_specs=[pl.BlockSpec((tm, d), lambda i: (i, 0)), pl.BlockSpec((1, d), lambda i: (0, 0)),
                  pl.BlockSpec((d, LANES), lambda i: (0, 0))],
        out_specs=pl.BlockSpec((tm, LANES), lambda i: (i, 0)),
        out_shape=jax.ShapeDtypeStruct((t, LANES), F32),
        compiler_params=_params("parallel"),
        name="gate_proj",
    )(x, g.reshape(1, d), jnp.pad(w, ((0, 0), (0, LANES - n))))
    return out[:, :n]


def _mm_kernel(x_ref, w_ref, o_ref):
    o_ref[...] = jnp.dot(x_ref[...], w_ref[...], preferred_element_type=F32).astype(o_ref.dtype)


def mm(x, w, tm, tn, out_dtype=F32):
    m, k = x.shape
    n = w.shape[1]
    assert m % tm == 0 and n % tn == 0, (m, n, tm, tn)
    return pl.pallas_call(
        _mm_kernel,
        grid=(n // tn, m // tm),
        in_specs=[pl.BlockSpec((tm, k), lambda j, i: (i, 0)), pl.BlockSpec((k, tn), lambda j, i: (0, j))],
        out_specs=pl.BlockSpec((tm, tn), lambda j, i: (i, j)),
        out_shape=jax.ShapeDtypeStruct((m, n), out_dtype),
        compiler_params=_params("parallel", "parallel"),
        name="mm",
    )(x, w)


def _mm_res_kernel(r_ref, *refs):
    o_ref = refs[-1]
    n_ops = (len(refs) - 1) // 2
    acc = r_ref[...]
    for x_ref, w_ref in zip(refs[:n_ops], refs[n_ops:2 * n_ops]):
        acc = acc + jnp.dot(x_ref[...], w_ref[...], preferred_element_type=F32)
    o_ref[...] = acc


def mm_residual(r, xs, ws, tm, tn):
    m, n = r.shape
    return pl.pallas_call(
        _mm_res_kernel,
        grid=(n // tn, m // tm),
        in_specs=[pl.BlockSpec((tm, tn), lambda j, i: (i, j))]
        + [pl.BlockSpec((tm, x.shape[1]), lambda j, i: (i, 0)) for x in xs]
        + [pl.BlockSpec((w.shape[0], tn), lambda j, i: (0, j)) for w in ws],
        out_specs=pl.BlockSpec((tm, tn), lambda j, i: (i, j)),
        out_shape=jax.ShapeDtypeStruct((m, n), F32),
        compiler_params=_params("parallel", "parallel"),
        name="mm_res",
    )(r, *xs, *ws)


def _pick(n, cands):
    for c in cands:
        if n % c == 0:
            return c
    raise ValueError(n)


def mm_tokens(x, w, out_dtype=F32):
    t = x.shape[0]
    n = w.shape[1]
    return mm(x, w, _pick(t, (512, 256, 128)), _pick(n, (1024, 512, 256, 128)), out_dtype)


def _topk_rows(arrays, k):
    n, lanes = arrays[0].shape
    iota = lax.broadcasted_iota(jnp.int32, (n, lanes), 0)
    iota_k = lax.broadcasted_iota(jnp.int32, (k, lanes), 0)
    unranked = jnp.full((n, lanes), k, jnp.int32)
    zeros_k = jnp.zeros((k, lanes), F32)

    def run(exact_ties):
        carries_rank = [exact_ties or i == len(arrays) - 1 for i in range(len(arrays))]

        def body(r, carry):
            out = []
            for (w, rank, vals), has_rank in zip(carry, carries_rank):
                mx = jnp.max(w, axis=0, keepdims=True)
                hit = w == mx
                if exact_ties:
                    hit = iota == jnp.min(jnp.where(hit, iota, n), axis=0, keepdims=True)
                out.append((jnp.where(hit, NEG_INF, w), jnp.where(hit, r, rank) if has_rank else rank,
                            jnp.where(iota_k == r, mx, vals)))
            return tuple(out)

        init = tuple((s, unranked if has_rank else 0, zeros_k) for s, has_rank in zip(arrays, carries_rank))
        res = []
        for s, (_, rank, vals), has_rank in zip(arrays, lax.fori_loop(0, k, body, init), carries_rank):
            if not has_rank:
                rank = unranked
                for r in range(k):
                    rank = jnp.where(s == vals[r:r + 1, :], r, rank)
            res.append((rank, vals))
        return tuple(res)

    fast = run(False)
    ranked = sum(jnp.sum((rank < k).astype(jnp.int32), axis=0, keepdims=True) for rank, _ in fast)
    clean = jnp.max(ranked) == k * len(arrays)
    res = lax.cond(clean, lambda: fast, lambda: run(True))
    return [(vals, rank) for rank, vals in res]


def _peer_select_kernel(q_ref, keys_ref, rank2_ref, e2z_ref, nsel_ref, e1_ref):
    k = PEER_TOPK
    half = k // 2
    L = q_ref.shape[1]
    dk2 = PEER_DK // 2
    sub = lax.broadcasted_iota(jnp.int32, (half, L), 0)

    def head(h, _):
        base = pl.multiple_of(h * PEER_DK, PEER_DK)
        s1 = jnp.dot(keys_ref[2 * h], q_ref[pl.ds(base, dk2), :], preferred_element_type=F32)
        s2 = jnp.dot(keys_ref[2 * h + 1], q_ref[pl.ds(base + dk2, dk2), :], preferred_element_type=F32)
        (v1, rank1), (v2, rank2) = _topk_rows([s1, s2], k)
        pieces, pos = [], []
        for r1 in range(half):
            pieces.append(v1[r1:r1 + 1, :] + v2[0:half, :])
            pos.append(sub + r1 * k)
        pieces.append(v1[0:1, :] + v2[half:k, :])
        pos.append(sub + half)
        pieces.append(v1[half:k, :] + v2[0:1, :])
        pos.append((sub + half) * k)
        cand = jnp.concatenate(pieces, axis=0)
        posid = jnp.concatenate(pos, axis=0)
        big = k * k

        def pick_all(exact_ties):
            def pick(_, w):
                mx = jnp.max(w, axis=0, keepdims=True)
                hit = w == mx
                if exact_ties:
                    hit = posid == jnp.min(jnp.where(hit, posid, big), axis=0, keepdims=True)
                return jnp.where(hit, NEG_INF, w)

            return (lax.fori_loop(0, k, pick, cand) == NEG_INF).astype(F32)

        sel_fast = pick_all(False)
        picked = jnp.sum(sel_fast, axis=0, keepdims=True)
        sel = lax.cond(jnp.max(picked) == k, lambda: sel_fast, lambda: pick_all(True))
        top = v1[0:1, :] + v2[0:1, :]
        z = jnp.sum(sel * jnp.exp(cand - top), axis=0, keepdims=True)
        counts = []
        for r1 in range(half):
            c = jnp.sum(sel[r1 * half:(r1 + 1) * half, :], axis=0, keepdims=True)
            if r1 == 0:
                c = c + jnp.sum(sel[half * half:half * half + half, :], axis=0, keepdims=True)
            counts.append(c)
        tail = sel[half * half + half:, :]
        for r1 in range(half, k):
            counts.append(tail[r1 - half:r1 - half + 1, :])
        nsel = jnp.zeros(s1.shape, F32)
        for r1 in range(k):
            nsel = jnp.where(rank1 == r1, counts[r1], nsel)
        rank2_ref[h] = rank2.astype(F32).astype(rank2_ref.dtype)
        nsel_ref[h] = nsel
        e1_ref[h] = jnp.exp(s1 - v1[0:1, :])
        e2z_ref[h] = (jnp.exp(s2 - v2[0:1, :]) / z).astype(e2z_ref.dtype)
        return 0

    lax.fori_loop(0, PEER_HEADS, head, 0)


def peer_select(q_t, keys):
    rows, t = q_t.shape
    spec = pl.BlockSpec((PEER_HEADS, PEER_NKEYS, LANES), lambda i: (0, 0, i))
    shp = jax.ShapeDtypeStruct((PEER_HEADS, PEER_NKEYS, t), F32)
    return pl.pallas_call(
        _peer_select_kernel,
        grid=(t // LANES,),
        in_specs=[pl.BlockSpec((rows, LANES), lambda i: (0, i)),
                  pl.BlockSpec(keys.shape, lambda i: (0, 0, 0))],
        out_specs=[spec] * 4,
        out_shape=[jax.ShapeDtypeStruct(shp.shape, BF16)] * 2 + [shp] * 2,
        compiler_params=_params("parallel"),
        name="peer_select",
    )(q_t, keys)


def _gelu(x):
    return 0.5 * x * (1.0 + lax.erf(x * (1.0 / math.sqrt(2.0))))


def _peer_dense_kernel(xn_ref, u_ref, vt_ref, rank2_ref, e2z_ref, nsel_ref, e1_ref, o_ref, a_ref):
    j = pl.program_id(1)
    nb = u_ref.shape[0] // PEER_NKEYS

    @pl.when(j == 0)
    def _():
        o_ref[...] = jnp.zeros_like(o_ref)

    for kb in range(nb):
        a = j * nb + kb
        rows = slice(kb * PEER_NKEYS, (kb + 1) * PEER_NKEYS)
        s = jnp.dot(u_ref[rows, :], xn_ref[...], preferred_element_type=F32)
        g = jnp.zeros(s.shape, BF16)
        for h in range(PEER_HEADS):
            n_row = jnp.broadcast_to(nsel_ref[h, pl.ds(a, 1), :], s.shape).astype(BF16)
            e_row = jnp.broadcast_to(e1_ref[h, pl.ds(a, 1), :], s.shape).astype(BF16)
            e2 = e2z_ref[h]
            g = g + jnp.where(rank2_ref[h] < n_row, e2, jnp.zeros_like(e2)) * e_row
        a_ref[rows, :] = (_gelu(s) * g.astype(F32)).astype(BF16)
    o_ref[...] += jnp.dot(vt_ref[...], a_ref[...], preferred_element_type=F32)


def peer_dense(xn_t, u_bf, vt_bf, rank2, e2z, nsel, e1, tm=512, te=512):
    d, t = xn_t.shape
    n_exp = u_bf.shape[0]
    sel_spec = pl.BlockSpec((PEER_HEADS, PEER_NKEYS, tm), lambda i, j: (0, 0, i))
    return pl.pallas_call(
        _peer_dense_kernel,
        grid=(t // tm, n_exp // te),
        in_specs=[pl.BlockSpec((d, tm), lambda i, j: (0, i)),
                  pl.BlockSpec((te, d), lambda i, j: (j, 0)),
                  pl.BlockSpec((d, te), lambda i, j: (0, j)),
                  sel_spec, sel_spec, sel_spec, sel_spec],
        out_specs=pl.BlockSpec((d, tm), lambda i, j: (0, i)),
        out_shape=jax.ShapeDtypeStruct((d, t), F32),
        scratch_shapes=[pltpu.VMEM((te, tm), BF16)],
        compiler_params=_params("parallel", "arbitrary"),
        name="peer_dense",
    )(xn_t, u_bf, vt_bf, rank2, e2z, nsel, e1)


def peer(x, g, w_q, sub_keys, u_tab, v_tab):
    t, d = x.shape
    xn_t = rmsnorm_bf16_t(x, g)
    wq_t = w_q.T.astype(BF16)
    q_t = mm(wq_t, xn_t, _pick(wq_t.shape[0], (512, 256, 128)), _pick(t, (512, 256, 128)), BF16)
    keys = sub_keys.reshape(PEER_HEADS * 2, PEER_NKEYS, PEER_DK // 2).astype(BF16)
    rank2, e2z, nsel, e1 = peer_select(q_t, keys)
    out_t = peer_dense(xn_t, u_tab.astype(BF16), cast_transposed_bf16(v_tab), rank2, e2z, nsel, e1,
                       tm=_pick(t, (512, 256, 128)))
    return add_transposed(x, out_t)


def _fox_norm_kernel(p_ref, gq_ref, gk_ref, q_ref, k_ref):
    scale = FOX_DH ** -0.5
    for h in range(FOX_HEADS):
        lo, hi = h * FOX_DH, (h + 1) * FOX_DH
        x = p_ref[:, lo:hi]
        y = x * lax.rsqrt(jnp.mean(x * x, axis=-1, keepdims=True) + RMS_EPS) * gq_ref[...]
        q_ref[:, lo:hi] = (y * scale).astype(q_ref.dtype)
        x = p_ref[:, FOX_W + lo:FOX_W + hi]
        k_ref[:, lo:hi] = x * lax.rsqrt(jnp.mean(x * x, axis=-1, keepdims=True) + RMS_EPS) * gk_ref[...]


def fox_norm(p, g_q, g_k):
    t = p.shape[0]
    tm = _pick(t, (256, 128))
    return pl.pallas_call(
        _fox_norm_kernel,
        grid=(t // tm,),
        in_specs=[pl.BlockSpec((tm, 2 * FOX_W), lambda i: (i, 0)),
                  pl.BlockSpec((1, FOX_DH), lambda i: (0, 0)), pl.BlockSpec((1, FOX_DH), lambda i: (0, 0))],
        out_specs=[pl.BlockSpec((tm, FOX_W), lambda i: (i, 0))] * 2,
        out_shape=[jax.ShapeDtypeStruct((t, FOX_W), BF16), jax.ShapeDtypeStruct((t, FOX_W), F32)],
        compiler_params=_params("parallel"),
        name="fox_norm",
    )(p, g_q.reshape(1, FOX_DH), g_k.reshape(1, FOX_DH))


def _fox_prompt_kernel(qi_ref, ki_ref, q_ref, k_ref, v_ref, f_ref, o_ref, m_ref, l_ref, acc_ref, *, tq, tk):
    n = pl.program_id(2)
    i = qi_ref[n]
    kk = ki_ref[n]

    @pl.when(kk == 0)
    def _():
        m_ref[...] = jnp.full_like(m_ref, NEG_INF)
        l_ref[...] = jnp.zeros_like(l_ref)
        acc_ref[...] = jnp.zeros_like(acc_ref)

    def step(masked):
        s = lax.dot_general(q_ref[...], k_ref[...].astype(BF16), (((1,), (1,)), ((), ())),
                            preferred_element_type=F32)
        s = s - f_ref[0]
        if masked:
            q_pos = i * tq + lax.broadcasted_iota(jnp.int32, s.shape, 0)
            k_pos = kk * tk + lax.broadcasted_iota(jnp.int32, s.shape, 1)
            s = jnp.where(q_pos >= k_pos, s, NEG_INF)
        m_prev = m_ref[...]
        m_new = jnp.maximum(m_prev, jnp.max(s, axis=1, keepdims=True))
        alpha = jnp.exp(m_prev - m_new)
        p = jnp.exp(s - m_new)
        l_ref[...] = alpha * l_ref[...] + jnp.sum(p, axis=1, keepdims=True)
        acc_ref[...] = alpha * acc_ref[...] + jnp.dot(p.astype(BF16), v_ref[...].astype(BF16),
                                                      preferred_element_type=F32)
        m_ref[...] = m_new

    on_diag = (kk + 1) * tk > i * tq + 1

    @pl.when(on_diag)
    def _():
        step(True)

    @pl.when(jnp.logical_not(on_diag))
    def _():
        step(False)

    @pl.when(kk == (i * tq + tq - 1) // tk)
    def _():
        o_ref[...] = (acc_ref[...] / l_ref[...]).astype(o_ref.dtype)


def fox_prompt(q_bf, k, p, f_cum, bsz, s_len, tq=1024, tk=1024):
    nq, nk = s_len // tq, s_len // tk
    pairs = [(i, kk) for i in range(nq) for kk in range((i * tq + tq - 1) // tk + 1)]
    qi = jnp.asarray([a for a, _ in pairs], jnp.int32)
    ki = jnp.asarray([b for _, b in pairs], jnp.int32)
    v_col0 = 2 * FOX_W // FOX_DH
    grid_spec = pltpu.PrefetchScalarGridSpec(
        num_scalar_prefetch=2,
        grid=(bsz, FOX_HEADS, len(pairs)),
        in_specs=[pl.BlockSpec((tq, FOX_DH), lambda b, h, n, qi, ki: (b * nq + qi[n], h)),
                  pl.BlockSpec((tk, FOX_DH), lambda b, h, n, qi, ki: (b * nk + ki[n], h)),
                  pl.BlockSpec((tk, FOX_DH), lambda b, h, n, qi, ki: (b * nk + ki[n], v_col0 + h)),
                  pl.BlockSpec((1, 1, tk), lambda b, h, n, qi, ki: (b * FOX_HEADS + h, 0, ki[n]))],
        out_specs=pl.BlockSpec((tq, FOX_DH), lambda b, h, n, qi, ki: (b * nq + qi[n], h)),
        scratch_shapes=[pltpu.VMEM((tq, 1), F32), pltpu.VMEM((tq, 1), F32), pltpu.VMEM((tq, FOX_DH), F32)],
    )
    return pl.pallas_call(
        functools.partial(_fox_prompt_kernel, tq=tq, tk=tk),
        grid_spec=grid_spec,
        out_shape=jax.ShapeDtypeStruct((bsz * s_len, FOX_W), BF16),
        compiler_params=_params("parallel", "parallel", "arbitrary"),
        name="fox_prompt",
    )(qi, ki, q_bf, k, p, f_cum)


SUB = 8


def _logf_cumsum_kernel(pt_ref, *refs):
    lf_refs, o_ref = refs[:-1], refs[-1]
    page, nh = lf_refs[0].shape[1:]
    tri = (lax.broadcasted_iota(jnp.int32, (page, page), 1)
           <= lax.broadcasted_iota(jnp.int32, (page, page), 0)).astype(F32)
    carry = jnp.zeros((1, nh), F32)
    for p, lf_ref in enumerate(lf_refs):
        f = jnp.dot(tri, lf_ref[0], precision=lax.Precision.HIGHEST, preferred_element_type=F32) + carry
        carry = f[page - 1:page, :]
        o_ref[0, p] = f


def paged_logf_cumsum(cache_logf, page_table):
    bsz, n_pages = page_table.shape
    _, page, nh = cache_logf.shape

    def page_spec(p):
        return pl.BlockSpec((1, page, nh), lambda b, pt: (pt[b, p], 0, 0))

    grid_spec = pltpu.PrefetchScalarGridSpec(
        num_scalar_prefetch=1,
        grid=(bsz,),
        in_specs=[page_spec(p) for p in range(n_pages)],
        out_specs=pl.BlockSpec((1, n_pages, page, nh), lambda b, pt: (b, 0, 0, 0)),
    )
    return pl.pallas_call(
        _logf_cumsum_kernel,
        grid_spec=grid_spec,
        out_shape=jax.ShapeDtypeStruct((bsz, n_pages, page, nh), F32),
        compiler_params=_params("parallel"),
        name="logf_cumsum",
    )(page_table, *([cache_logf] * n_pages))


def _fox_sample_kernel(pt_ref, q_ref, *refs, n_groups, group):
    kc_refs, vc_refs = refs[:group], refs[group:2 * group]
    f_ref, mask_ref, kn_ref, vn_ref, fn_ref, maskn_ref, o_ref, m_ref, l_ref, acc_ref = refs[2 * group:]
    pg = pl.program_id(1)

    @pl.when(pg == 0)
    def _():
        m_ref[...] = jnp.full_like(m_ref, NEG_INF)
        l_ref[...] = jnp.zeros_like(l_ref)
        acc_ref[...] = jnp.zeros_like(acc_ref)

    def attend(kvb):
        ss = [lax.dot_general(q_ref[0], k2, (((1,), (1,)), ((), ())), preferred_element_type=F32) + b
              for k2, _, b in kvb]
        m_prev = m_ref[...]
        m_new = m_prev
        for s in ss:
            m_new = jnp.maximum(m_new, jnp.max(s, axis=1, keepdims=True))
        alpha = jnp.exp(m_prev - m_new)
        l_new = alpha * l_ref[...]
        acc = alpha * acc_ref[...]
        for s, (_, v2, _) in zip(ss, kvb):
            p = jnp.exp(s - m_new)
            l_new = l_new + jnp.sum(p, axis=1, keepdims=True)
            acc = acc + jnp.dot(p.astype(BF16), v2, preferred_element_type=F32)
        l_ref[...] = l_new
        acc_ref[...] = acc
        m_ref[...] = m_new

    @pl.when(pg < n_groups)
    def _():
        rows = kc_refs[0].shape[0] * kc_refs[0].shape[1]
        attend([(kc[...].reshape(rows, FOX_DH).astype(BF16), vc[...].reshape(rows, FOX_DH).astype(BF16),
                 mask_ref[...] - f_ref[0, g])
                for g, (kc, vc) in enumerate(zip(kc_refs, vc_refs))])

    @pl.when(pg == n_groups)
    def _():
        attend([(kn_ref[0].astype(BF16), vn_ref[0].astype(BF16), maskn_ref[...] - fn_ref[0])])
        o_ref[0] = acc_ref[...] / l_ref[...]


def fox_sample(q_bf, k_new, v_new, lf_new, cache_k, cache_v, cache_logf, page_table):
    bsz, t_len, _ = q_bf.shape
    n_pool, page = cache_k.shape[:2]
    n_pages = page_table.shape[1]
    nh, dh = FOX_HEADS, FOX_DH
    rows = nh * t_len
    lanes = page * nh
    lanes_new = SUB * nh
    qf = q_bf.reshape(bsz, t_len, nh, dh).transpose(0, 2, 1, 3).reshape(bsz, rows, dh)
    f_pages = paged_logf_cumsum(cache_logf, page_table)
    f_rows = f_pages.reshape(bsz, n_pages, 1, lanes)
    pad = ((0, 0), (0, SUB - t_len), (0, 0))
    kn = jnp.pad(k_new, pad).reshape(bsz, lanes_new, dh)
    vn = jnp.pad(v_new, pad).reshape(bsz, lanes_new, dh)
    fn = (f_pages[:, -1, -1, :][:, None, :] + jnp.cumsum(jnp.pad(lf_new, pad), axis=1)).reshape(bsz, 1, lanes_new)
    row_head, row_q = np.arange(rows) // t_len, np.arange(rows) % t_len
    mask = np.where(row_head[:, None] == (np.arange(lanes) % nh)[None, :], 0.0, NEG_INF).astype(np.float32)
    key_n, head_n = np.arange(lanes_new) // nh, np.arange(lanes_new) % nh
    ok = (row_head[:, None] == head_n[None, :]) & (key_n[None, :] <= row_q[:, None]) & (key_n[None, :] < t_len)
    mask_new = np.where(ok, 0.0, NEG_INF).astype(np.float32)
    group = _pick(n_pages, (4, 2, 1))
    n_groups = n_pages // group
    last = n_groups - 1

    def cache_spec(g):
        return pl.BlockSpec((page, nh, dh), lambda b, p, pt: (pt[b, jnp.minimum(p, last) * group + g], 0, 0))

    cache_specs = [cache_spec(g) for g in range(group)]
    kc3 = cache_k.reshape(n_pool * page, nh, dh)
    vc3 = cache_v.reshape(n_pool * page, nh, dh)
    grid_spec = pltpu.PrefetchScalarGridSpec(
        num_scalar_prefetch=1,
        grid=(bsz, n_groups + 1),
        in_specs=[pl.BlockSpec((1, rows, dh), lambda b, p, pt: (b, 0, 0)),
                  *cache_specs, *cache_specs,
                  pl.BlockSpec((1, group, 1, lanes), lambda b, p, pt: (b, jnp.minimum(p, last), 0, 0)),
                  pl.BlockSpec((rows, lanes), lambda b, p, pt: (0, 0)),
                  pl.BlockSpec((1, lanes_new, dh), lambda b, p, pt: (b, 0, 0)),
                  pl.BlockSpec((1, lanes_new, dh), lambda b, p, pt: (b, 0, 0)),
                  pl.BlockSpec((1, 1, lanes_new), lambda b, p, pt: (b, 0, 0)),
                  pl.BlockSpec((rows, lanes_new), lambda b, p, pt: (0, 0))],
        out_specs=pl.BlockSpec((1, rows, dh), lambda b, p, pt: (b, 0, 0)),
        scratch_shapes=[pltpu.VMEM((rows, 1), F32), pltpu.VMEM((rows, 1), F32), pltpu.VMEM((rows, dh), F32)],
    )
    out = pl.pallas_call(
        functools.partial(_fox_sample_kernel, n_groups=n_groups, group=group),
        grid_spec=grid_spec,
        out_shape=jax.ShapeDtypeStruct((bsz, rows, dh), F32),
        compiler_params=_params("parallel", "arbitrary"),
        name="fox_sample",
    )(page_table, qf, *([kc3] * group), *([vc3] * group),
      f_rows, jnp.asarray(mask), kn, vn, fn, jnp.asarray(mask_new))
    return out.reshape(bsz, nh, t_len, dh).transpose(0, 2, 1, 3).reshape(bsz, t_len, nh * dh)


def _mlstm_kernel(q_ref, k_ref, v_ref, o_ref, g_ref, c0_ref, n0_ref, m0_ref, mg_ref,
                  h_ref, c1_ref, n1_ref, m1_ref, c_scr, n_scr, m_scr):
    c = pl.program_id(1)
    L = q_ref.shape[0]
    nh, dh = MLSTM_HEADS, MLSTM_DH

    @pl.when(c == 0)
    def _():
        c_scr[...] = c0_ref[0]
        n_scr[...] = n0_ref[0]
        m_scr[...] = m0_ref[0]

    t_idx = lax.broadcasted_iota(jnp.int32, (L, L), 0)
    s_idx = lax.broadcasted_iota(jnp.int32, (L, L), 1)
    causal = s_idx <= t_idx
    diag = s_idx == t_idx
    for h in range(nh):
        cols = slice(h * dh, (h + 1) * dh)
        ig = g_ref[0, h:h + 1, :]
        lf = g_ref[0, nh + h:nh + h + 1, :]
        m0 = m_scr[h:h + 1, :]
        f_col = jnp.sum(jnp.where(causal, lf, 0.0), axis=1, keepdims=True)
        ig_col = jnp.sum(jnp.where(diag, ig, 0.0), axis=1, keepdims=True)
        g_col = ig_col - f_col
        g_row = jnp.sum(jnp.where(diag, g_col, 0.0), axis=0, keepdims=True)
        cm_col = jnp.max(jnp.where(causal, g_row, NEG_INF), axis=1, keepdims=True)
        m_col = f_col + jnp.maximum(m0, cm_col)
        b_col = f_col - m_col
        a_col = jnp.exp(f_col + m0 - m_col)
        d = jnp.exp(jnp.where(causal, b_col + g_row, NEG_INF))
        q = q_ref[:, cols] * (dh ** -0.5)
        k = k_ref[:, cols]
        v = v_ref[:, cols]
        qb, kb, vb = q.astype(BF16), k.astype(BF16), v.astype(BF16)
        qk = lax.dot_general(qb, kb, (((1,), (1,)), ((), ())), preferred_element_type=F32) * d
        c_old = c_scr[h]
        n_old = n_scr[h:h + 1, :]
        num = a_col * jnp.dot(qb, c_old.astype(BF16), preferred_element_type=F32) \
            + jnp.dot(qk.astype(BF16), vb, preferred_element_type=F32)
        den = a_col * jnp.sum(q * n_old, axis=1, keepdims=True) + jnp.sum(qk, axis=1, keepdims=True)
        hs = num / jnp.maximum(jnp.abs(den), jnp.exp(-m_col))
        hn = hs * lax.rsqrt(jnp.mean(hs * hs, axis=1, keepdims=True) + RMS_EPS) * mg_ref[:, cols]
        h_ref[:, cols] = (hn * jax.nn.sigmoid(o_ref[:, cols])).astype(h_ref.dtype)
        b_last = b_col[L - 1:L, :]
        a_last = a_col[L - 1:L, :]
        kd = k * jnp.exp(b_last + g_col)
        c_scr[h] = a_last * c_old + lax.dot_general(kd.astype(BF16), vb, (((0,), (0,)), ((), ())),
                                                     preferred_element_type=F32)
        n_scr[h:h + 1, :] = a_last * n_old + jnp.sum(kd, axis=0, keepdims=True)
        m_scr[h:h + 1, :] = m_col[L - 1:L, :]

    @pl.when(c == pl.num_programs(1) - 1)
    def _():
        c1_ref[0] = c_scr[...]
        n1_ref[0] = n_scr[...]
        m1_ref[0] = m_scr[...]


def mlstm(p, gates, c0, n0, m0, mh_g, bsz, n_chunks, chunk, col0):
    nh, dh, mw = MLSTM_HEADS, MLSTM_DH, MLSTM_W
    rows = bsz * n_chunks * chunk

    def col_spec(j):
        return pl.BlockSpec((chunk, mw), lambda b, c: (b * n_chunks + c, col0 + j))

    state = lambda b, c: (b, 0, 0)
    return pl.pallas_call(
        _mlstm_kernel,
        grid=(bsz, n_chunks),
        in_specs=[col_spec(0), col_spec(1), col_spec(2), col_spec(3),
                  pl.BlockSpec((1, 2 * nh, chunk), lambda b, c: (b * n_chunks + c, 0, 0)),
                  pl.BlockSpec((1, nh, dh, dh), lambda b, c: (b, 0, 0, 0)),
                  pl.BlockSpec((1, nh, dh), state), pl.BlockSpec((1, nh, 1), state),
                  pl.BlockSpec((1, mw), lambda b, c: (0, 0))],
        out_specs=[pl.BlockSpec((chunk, mw), lambda b, c: (b * n_chunks + c, 0)),
                   pl.BlockSpec((1, nh, dh, dh), lambda b, c: (b, 0, 0, 0)),
                   pl.BlockSpec((1, nh, dh), state), pl.BlockSpec((1, nh, 1), state)],
        out_shape=[jax.ShapeDtypeStruct((rows, mw), BF16),
                   jax.ShapeDtypeStruct((bsz, nh, dh, dh), F32),
                   jax.ShapeDtypeStruct((bsz, nh, dh), F32),
                   jax.ShapeDtypeStruct((bsz, nh, 1), F32)],
        scratch_shapes=[pltpu.VMEM((nh, dh, dh), F32), pltpu.VMEM((nh, dh), F32), pltpu.VMEM((nh, 1), F32)],
        compiler_params=_params("parallel", "arbitrary"),
        name="mlstm",
    )(p, p, p, p, gates, c0, n0, m0.reshape(bsz, nh, 1), mh_g.reshape(1, mw))


CONV_HALO = 32
CONV_ROWS = 32


def _group_ln_swish(y, g, b):
    outs = []
    for grp in range(CONV_GROUPS):
        x = y[:, grp * LANES:(grp + 1) * LANES]
        mu = jnp.mean(x, axis=1, keepdims=True)
        xc = x - mu
        var = jnp.mean(xc * xc, axis=1, keepdims=True)
        z = xc * lax.rsqrt(var + LN_EPS) * g[:, grp * LANES:(grp + 1) * LANES] + b[:, grp * LANES:(grp + 1) * LANES]
        outs.append(z * jax.nn.sigmoid(z))
    return jnp.concatenate(outs, axis=1)


def _conv_prompt_kernel(cv_ref, cg_ref, hv_ref, hg_ref, w_ref, b_ref, g_ref, beta_ref, y_ref, buf_ref, ext_ref):
    i = pl.program_id(1)
    ts = cv_ref.shape[0]
    hist = jnp.where(i > 0, hv_ref[...] * jax.nn.sigmoid(hg_ref[...]), 0.0)
    ext_ref[0:CONV_HALO, :] = hist
    ext_ref[CONV_HALO:, :] = cv_ref[...] * jax.nn.sigmoid(cg_ref[...])
    off = CONV_HALO - (CONV_K - 1)
    for r0 in range(0, ts, CONV_ROWS):
        acc = jnp.zeros((CONV_ROWS, CONV_W), F32) + b_ref[...]
        for j in range(CONV_K):
            acc = acc + ext_ref[r0 + off + j:r0 + off + j + CONV_ROWS, :] * w_ref[j:j + 1, :]
        y_ref[r0:r0 + CONV_ROWS, :] = _group_ln_swish(acc, g_ref[...], beta_ref[...]).astype(y_ref.dtype)

    @pl.when(i == pl.num_programs(1) - 1)
    def _():
        buf_ref[0] = ext_ref[ts + off:ts + CONV_HALO, :]


def conv_prompt(p, conv_w, conv_b, cn_g, cn_b, bsz, s_len, ts=512):
    nt = s_len // ts
    hb = ts // CONV_HALO

    def halo(col):
        return pl.BlockSpec((CONV_HALO, CONV_W), lambda b, i: (jnp.maximum((b * nt + i) * hb - 1, 0), col))

    vec = pl.BlockSpec((1, CONV_W), lambda b, i: (0, 0))
    return pl.pallas_call(
        _conv_prompt_kernel,
        grid=(bsz, nt),
        in_specs=[pl.BlockSpec((ts, CONV_W), lambda b, i: (b * nt + i, 0)),
                  pl.BlockSpec((ts, CONV_W), lambda b, i: (b * nt + i, 1)),
                  halo(0), halo(1),
                  pl.BlockSpec((CONV_K, CONV_W), lambda b, i: (0, 0)), vec, vec, vec],
        out_specs=[pl.BlockSpec((ts, CONV_W), lambda b, i: (b * nt + i, 0)),
                   pl.BlockSpec((1, CONV_K - 1, CONV_W), lambda b, i: (b, 0, 0))],
        out_shape=[jax.ShapeDtypeStruct((bsz * s_len, CONV_W), BF16),
                   jax.ShapeDtypeStruct((bsz, CONV_K - 1, CONV_W), F32)],
        scratch_shapes=[pltpu.VMEM((ts + CONV_HALO, CONV_W), F32)],
        compiler_params=_params("parallel", "arbitrary"),
        name="conv_prompt",
    )(p, p, p, p, conv_w, conv_b.reshape(1, -1), cn_g.reshape(1, -1), cn_b.reshape(1, -1))


def _conv_sample_kernel(cv_ref, cg_ref, st_ref, w_ref, b_ref, g_ref, beta_ref, y_ref, buf_ref, ext_ref, *, t_len):
    nb = st_ref.shape[0]
    hist = CONV_K - 1
    y_ref[...] = jnp.zeros_like(y_ref)
    for bb in range(nb):
        rows = slice(bb * SUB, bb * SUB + t_len)
        ext_ref[0:hist, :] = st_ref[bb]
        ext_ref[hist:hist + t_len, :] = cv_ref[rows, :] * jax.nn.sigmoid(cg_ref[rows, :])
        ys = [jnp.sum(ext_ref[t:t + CONV_K, :] * w_ref[...], axis=0, keepdims=True) for t in range(t_len)]
        y = jnp.concatenate(ys, axis=0) + b_ref[...]
        y_ref[rows, :] = _group_ln_swish(y, g_ref[...], beta_ref[...]).astype(y_ref.dtype)
        buf_ref[bb] = ext_ref[t_len:t_len + hist, :]


def conv_sample(p_pad, state_conv, conv_w, conv_b, cn_g, cn_b, t_len, nb=8):
    bsz = state_conv.shape[0]
    vec = pl.BlockSpec((1, CONV_W), lambda i: (0, 0))
    return pl.pallas_call(
        functools.partial(_conv_sample_kernel, t_len=t_len),
        grid=(bsz // nb,),
        in_specs=[pl.BlockSpec((nb * SUB, CONV_W), lambda i: (i, 0)),
                  pl.BlockSpec((nb * SUB, CONV_W), lambda i: (i, 1)),
                  pl.BlockSpec((nb, CONV_K - 1, CONV_W), lambda i: (i, 0, 0)),
                  pl.BlockSpec((CONV_K, CONV_W), lambda i: (0, 0)), vec, vec, vec],
        out_specs=[pl.BlockSpec((nb * SUB, CONV_W), lambda i: (i, 0)),
                   pl.BlockSpec((nb, CONV_K - 1, CONV_W), lambda i: (i, 0, 0))],
        out_shape=[jax.ShapeDtypeStruct((bsz * SUB, CONV_W), BF16),
                   jax.ShapeDtypeStruct((bsz, CONV_K - 1, CONV_W), F32)],
        scratch_shapes=[pltpu.VMEM((CONV_K - 1 + SUB, CONV_W), F32)],
        compiler_params=_params("parallel"),
        name="conv_sample",
    )(p_pad, p_pad, state_conv, conv_w, conv_b.reshape(1, -1), cn_g.reshape(1, -1), cn_b.reshape(1, -1))


def kernel(x_prompt, x_sample, state_conv, state_C, state_n, state_m, cache_k, cache_v, cache_logf, page_table,
           norm0_mix, w_in0, b_ig0, b_fg0, conv_w0, conv_b0, conv_norm_g0, conv_norm_b0, mlstm_norm_g0, w_out0,
           norm1_mix, w_in1, b_f1, q_norm_g1, k_norm_g1, w_out1,
           norm0_ffn, peer_wq0, peer_keys0, peer_u0, peer_v0,
           norm1_ffn, peer_wq1, peer_keys1, peer_u1, peer_v1):
    bp, sp, d = x_prompt.shape
    bs, ss, _ = x_sample.shape
    tp, ts = bp * sp, bs * ss
    x = jnp.concatenate([x_prompt.reshape(tp, d), x_sample.reshape(ts, d)], axis=0)

    n_main0 = 2 * CONV_W + 4 * MLSTM_W
    xn = rmsnorm_bf16(x, norm0_mix)
    p0 = mm_tokens(xn, w_in0[:, :n_main0].astype(BF16))
    g0 = gate_proj(x, norm0_mix, w_in0[:, n_main0:])
    nh = MLSTM_HEADS
    gates = jnp.concatenate([g0[:, :nh] + b_ig0, jax.nn.log_sigmoid(g0[:, nh:] + b_fg0)], axis=1)
    conv_par = (conv_w0, conv_b0, conv_norm_g0, conv_norm_b0)
    mcol0 = 2 * CONV_W // MLSTM_W
    nc = sp // MLSTM_CHUNK
    gates_p = gates[:tp].reshape(bp * nc, MLSTM_CHUNK, 2 * nh).transpose(0, 2, 1)
    hp, p_C, p_n, p_m = mlstm(p0, gates_p, jnp.zeros((bp, nh, MLSTM_DH, MLSTM_DH), F32),
                              jnp.zeros((bp, nh, MLSTM_DH), F32), jnp.zeros((bp, nh), F32),
                              mlstm_norm_g0, bp, nc, MLSTM_CHUNK, mcol0)
    yp, p_conv = conv_prompt(p0, *conv_par, bp, sp)
    tpad = ((0, 0), (0, SUB - ss), (0, 0))
    p0s = jnp.pad(p0[tp:].reshape(bs, ss, -1), tpad).reshape(bs * SUB, -1)
    gs = gates[tp:].reshape(bs, ss, 2 * nh)
    gates_s = jnp.concatenate([jnp.pad(gs[..., :nh], tpad, constant_values=NEG_INF),
                               jnp.pad(gs[..., nh:], tpad)], axis=-1).transpose(0, 2, 1)
    hs, s_C, s_n, s_m = mlstm(p0s, gates_s, state_C, state_n, state_m, mlstm_norm_g0, bs, 1, SUB, mcol0)
    ys, s_conv = conv_sample(p0s, state_conv, *conv_par, ss)
    unpad = lambda a: a.reshape(bs, SUB, -1)[:, :ss].reshape(ts, -1)
    y_all = jnp.concatenate([yp, unpad(ys)], axis=0)
    h_all = jnp.concatenate([hp[:tp], unpad(hs)], axis=0)
    w_out0_bf = w_out0.astype(BF16)
    tm_out = _pick(tp + ts, (512, 256, 128))
    x = mm_residual(x, [y_all, h_all], [w_out0_bf[:CONV_W], w_out0_bf[CONV_W:]], tm_out, 1024)
    p_m, s_m = p_m.reshape(bp, nh), s_m.reshape(bs, nh)
    x = peer(x, norm0_ffn, peer_wq0, peer_keys0, peer_u0, peer_v0)

    xn = rmsnorm_bf16(x, norm1_mix)
    p1 = mm_tokens(xn, w_in1[:, :3 * FOX_W].astype(BF16))
    g1 = gate_proj(x, norm1_mix, w_in1[:, 3 * FOX_W:])
    lf = jax.nn.log_sigmoid(g1 + b_f1)
    q_bf, k = fox_norm(p1, q_norm_g1, k_norm_g1)
    v = p1[:, 2 * FOX_W:]
    f_cum = jnp.cumsum(lf[:tp].reshape(bp, sp, FOX_HEADS), axis=1).transpose(0, 2, 1).reshape(bp * FOX_HEADS, 1, sp)
    attn_p = fox_prompt(q_bf, k, p1, f_cum, bp, sp)
    attn_s = fox_sample(q_bf[tp:].reshape(bs, ss, FOX_W), k[tp:].reshape(bs, ss, FOX_W),
                        v[tp:].reshape(bs, ss, FOX_W), lf[tp:].reshape(bs, ss, FOX_HEADS),
                        cache_k, cache_v, cache_logf, page_table)
    attn = jnp.concatenate([attn_p, attn_s.reshape(ts, FOX_W).astype(BF16)], axis=0)
    x = mm_residual(x, [attn], [w_out1.astype(BF16)], tm_out, 1024)
    x = peer(x, norm1_ffn, peer_wq1, peer_keys1, peer_u1, peer_v1)

    hd = (FOX_HEADS, FOX_DH)
    return (x[:tp].reshape(bp, sp, d), x[tp:].reshape(bs, ss, d), p_conv, p_C, p_n, p_m,
            k[:tp].reshape(bp, sp, *hd), v[:tp].reshape(bp, sp, *hd), lf[:tp].reshape(bp, sp, FOX_HEADS),
            s_conv, s_C, s_n, s_m,
            k[tp:].reshape(bs, ss, *hd), v[tp:].reshape(bs, ss, *hd), lf[tp:].reshape(bs, ss, FOX_HEADS))
```
